```python
import jax, jax.numpy as jnp
from jax import lax
import numpy as np

D_MODEL = 1024
BATCH = 32
SEQ = 2048
DEPTH = 1

SB_HEADS = 8
SB_HEAD_DIM = 64
MLA_HEADS = 8
QK_NOPE_DIM = 64
QK_ROPE_DIM = 32
V_HEAD_DIM = 64
Q_LORA_RANK = 256
KV_LORA_RANK = 128
SB_WIDTH = SB_HEADS * SB_HEAD_DIM
MLA_WIDTH = MLA_HEADS * V_HEAD_DIM
D_MIX = SB_WIDTH + MLA_WIDTH
IN_COLS = 3 * SB_WIDTH + Q_LORA_RANK + KV_LORA_RANK + QK_ROPE_DIM
D_FF = 2816
CONV_WIDTH = 3
PLE_DIM = 256
Q_BLOCK = 128
ROPE_THETA = 10000.0
NORM_EPS = 1e-6
MAX_POS_OFFSET = 4096

kernel_name = 'hybrid_stickbreak_mla_block'


def rms_norm(x, gain):
    xf = x.astype(jnp.float32)
    y = xf * lax.rsqrt(jnp.mean(xf * xf, axis=-1, keepdims=True) + NORM_EPS)
    return (y * gain.astype(jnp.float32)).astype(x.dtype)


def rope_tables(positions, dtype):
    inv_freq = 1.0 / (ROPE_THETA ** (jnp.arange(0, QK_ROPE_DIM, 2, dtype=jnp.float32) / QK_ROPE_DIM))
    ang = positions.astype(jnp.float32)[..., None] * inv_freq
    return jnp.cos(ang).astype(dtype), jnp.sin(ang).astype(dtype)


def apply_rope(x, cos, sin):
    x1, x2 = jnp.split(x, 2, axis=-1)
    return jnp.concatenate([x1 * cos - x2 * sin, x2 * cos + x1 * sin], axis=-1)


def stick_breaking_attention(q, k, v):
    seq = q.shape[1]
    scale = SB_HEAD_DIM ** -0.5
    outs = []
    for start in range(0, seq, Q_BLOCK):
        end = start + Q_BLOCK
        z = jnp.einsum('bqhd,bkhd->bhqk', q[:, start:end], k[:, :end]).astype(jnp.float32) * scale
        t_idx = start + jnp.arange(Q_BLOCK)[:, None]
        s_idx = jnp.arange(end)[None, :]
        strict = s_idx < t_idx
        log_keep = jnp.where(strict, -jax.nn.softplus(z), 0.0)
        log_keep_after = lax.cumsum(log_keep, axis=3, reverse=True) - log_keep
        weights = jnp.where(strict, jnp.exp(jax.nn.log_sigmoid(z) + log_keep_after), 0.0)
        outs.append(jnp.einsum('bhqk,bkhd->bqhd', weights.astype(v.dtype), v[:, :end]))
    return jnp.concatenate(outs, axis=1)


def mla_attention(q_nope, q_rope, k_nope, k_rope, v):
    seq = q_nope.shape[1]
    scale = (QK_NOPE_DIM + QK_ROPE_DIM) ** -0.5
    outs = []
    for start in range(0, seq, Q_BLOCK):
        end = start + Q_BLOCK
        s = (jnp.einsum('bqhd,bkhd->bhqk', q_nope[:, start:end], k_nope[:, :end])
             + jnp.einsum('bqhd,bkd->bhqk', q_rope[:, start:end], k_rope[:, :end]))
        s = s.astype(jnp.float32) * scale
        causal = jnp.arange(end)[None, :] <= (start + jnp.arange(Q_BLOCK)[:, None])
        probs = jax.nn.softmax(jnp.where(causal, s, -jnp.inf), axis=-1)
        outs.append(jnp.einsum('bhqk,bkhd->bqhd', probs.astype(v.dtype), v[:, :end]))
    return jnp.concatenate(outs, axis=1)


def causal_depthwise_conv(u, w, b):
    c = u.shape[-1]
    y = lax.conv_general_dilated(u, w[:, None, :].astype(u.dtype), window_strides=(1,),
                                 padding=[(CONV_WIDTH - 1, 0)],
                                 dimension_numbers=('NWC', 'WIO', 'NWC'),
                                 feature_group_count=c)
    return y + b.astype(u.dtype)


def setup_inputs(seed: int = 0) -> dict:
    key = jax.random.key(seed)
    ks = jax.random.split(key, 24)
    f32 = jnp.float32

    def normal(k, shape, scale):
        return jax.random.normal(k, shape, f32) * scale

    def gain(k, shape):
        return 1.0 + 0.05 * jax.random.normal(k, shape, f32)

    L = DEPTH
    x = normal(ks[0], (BATCH, SEQ, D_MODEL), 1.0)
    p = normal(ks[1], (DEPTH, BATCH, SEQ, PLE_DIM), 1.0)
    offsets = jax.random.randint(ks[2], (BATCH, 1), 0, MAX_POS_OFFSET, dtype=jnp.int32)
    positions = offsets + jnp.arange(SEQ, dtype=jnp.int32)[None, :]
    return {
        'x': x,
        'p': p,
        'positions': positions,
        'w_in': normal(ks[3], (L, D_MODEL, IN_COLS), D_MODEL ** -0.5),
        'g_pre_mix': gain(ks[4], (L, D_MODEL)),
        'g_q_lat': gain(ks[5], (L, Q_LORA_RANK)),
        'w_q_up': normal(ks[6], (L, Q_LORA_RANK, MLA_HEADS * (QK_NOPE_DIM + QK_ROPE_DIM)), Q_LORA_RANK ** -0.5),
        'g_kv_lat': gain(ks[7], (L, KV_LORA_RANK)),
        'w_kv_up': normal(ks[8], (L, KV_LORA_RANK, MLA_HEADS * (QK_NOPE_DIM + V_HEAD_DIM)), KV_LORA_RANK ** -0.5),
        'g_grp_sb': gain(ks[9], (L, SB_WIDTH)),
        'g_grp_mla': gain(ks[10], (L, MLA_WIDTH)),
        'w_o': normal(ks[11], (L, D_MIX, D_MODEL), D_MIX ** -0.5),
        'g_post_mix': gain(ks[12], (L, D_MODEL)),
        'g_pre_ffn': gain(ks[13], (L, D_MODEL)),
        'w_up': normal(ks[14], (L, D_MODEL, 2 * D_FF), D_MODEL ** -0.5),
        'conv_w': normal(ks[15], (L, CONV_WIDTH, 2 * D_FF), CONV_WIDTH ** -0.5),
        'conv_b': normal(ks[16], (L, 2 * D_FF), 0.01),
        'w_down': normal(ks[17], (L, D_FF, D_MODEL), D_FF ** -0.5),
        'g_post_ffn': gain(ks[18], (L, D_MODEL)),
        'w_ple': normal(ks[19], (L, PLE_DIM, D_MODEL), PLE_DIM ** -0.5),
        'g_ple_gate': gain(ks[20], (L, D_MODEL)),
        'w_ple_gate': normal(ks[21], (L, D_MODEL, D_MODEL), D_MODEL ** -0.5),
        'b_ple_gate': normal(ks[22], (L, D_MODEL), 0.01),
        'g_post_ple': gain(ks[23], (L, D_MODEL)),
    }


def reference(x, p, positions, w_in, g_pre_mix, g_q_lat, w_q_up, g_kv_lat, w_kv_up,
              g_grp_sb, g_grp_mla, w_o, g_post_mix, g_pre_ffn, w_up, conv_w, conv_b,
              w_down, g_post_ffn, w_ple, g_ple_gate, w_ple_gate, b_ple_gate, g_post_ple):
    b, s, _ = x.shape
    cos, sin = rope_tables(positions, x.dtype)
    split_at = [SB_WIDTH, 2 * SB_WIDTH, 3 * SB_WIDTH,
                3 * SB_WIDTH + Q_LORA_RANK, 3 * SB_WIDTH + Q_LORA_RANK + KV_LORA_RANK]
    for i in range(DEPTH):
        h = rms_norm(x, g_pre_mix[i])
        proj = h @ w_in[i]
        sb_q, sb_k, sb_v, q_lat, kv_lat, k_rope = jnp.split(proj, split_at, axis=-1)

        o_sb = stick_breaking_attention(sb_q.reshape(b, s, SB_HEADS, SB_HEAD_DIM),
                                        sb_k.reshape(b, s, SB_HEADS, SB_HEAD_DIM),
                                        sb_v.reshape(b, s, SB_HEADS, SB_HEAD_DIM)).reshape(b, s, SB_WIDTH)

        q = (rms_norm(q_lat, g_q_lat[i]) @ w_q_up[i]).reshape(b, s, MLA_HEADS, QK_NOPE_DIM + QK_ROPE_DIM)
        q_nope, q_rope = jnp.split(q, [QK_NOPE_DIM], axis=-1)
        q_rope = apply_rope(q_rope, cos[:, :, None, :], sin[:, :, None, :])
        kv = (rms_norm(kv_lat, g_kv_lat[i]) @ w_kv_up[i]).reshape(b, s, MLA_HEADS, QK_NOPE_DIM + V_HEAD_DIM)
        k_nope, v_mla = jnp.split(kv, [QK_NOPE_DIM], axis=-1)
        k_rope = apply_rope(k_rope, cos, sin)
        o_mla = mla_attention(q_nope, q_rope, k_nope, k_rope, v_mla).reshape(b, s, MLA_WIDTH)

        mix = jnp.concatenate([rms_norm(o_sb, g_grp_sb[i]), rms_norm(o_mla, g_grp_mla[i])], axis=-1) @ w_o[i]
        x = x + rms_norm(mix, g_post_mix[i])

        h = rms_norm(x, g_pre_ffn[i])
        u = causal_depthwise_conv(h @ w_up[i], conv_w[i], conv_b[i])
        gate, val = jnp.split(u, 2, axis=-1)
        f = (jax.nn.gelu(gate, approximate=True) * val) @ w_down[i]
        x = x + rms_norm(f, g_post_ffn[i])

        e = p[i] @ w_ple[i]
        g = jax.nn.sigmoid(rms_norm(x, g_ple_gate[i]) @ w_ple_gate[i] + b_ple_gate[i])
        x = x + rms_norm(g * e, g_post_ple[i])
    return x
```

```python
import functools

import jax
import jax.numpy as jnp
from jax import lax
from jax.experimental import pallas as pl
from jax.experimental.pallas import tpu as pltpu

F32 = jnp.float32
BF16 = jnp.bfloat16

D_MODEL = 1024
SB_HEADS = 8
SB_HEAD_DIM = 64
MLA_HEADS = 8
QK_NOPE_DIM = 64
QK_ROPE_DIM = 32
V_HEAD_DIM = 64
Q_LORA_RANK = 256
KV_LORA_RANK = 128
SB_WIDTH = SB_HEADS * SB_HEAD_DIM
MLA_WIDTH = MLA_HEADS * V_HEAD_DIM
D_FF = 2816
CONV_WIDTH = 3
PLE_DIM = 256
ROPE_THETA = 10000.0
NORM_EPS = 1e-6

LANES = 128
MLA_QK_PAD = LANES
HEAD_PAIRS = SB_HEADS // 2
SB_SCALE = SB_HEAD_DIM ** -0.5
MLA_SCALE = (QK_NOPE_DIM + QK_ROPE_DIM) ** -0.5

PROJ_TM = 512
ATT_TQ = 256
ATT_TK = 256
MIX_TM = 512
FFN_TM = 512
FFN_CH = 256
CONV_HALO = 8
VMEM_LIMIT = 56 * 1024 * 1024


def _rms(x, g):
    ms = jnp.mean(x * x, axis=-1, keepdims=True)
    return x * lax.rsqrt(ms + NORM_EPS) * g


def _dot(a, b):
    return jnp.dot(a, b, preferred_element_type=F32)


def _dot_nt(a, b):
    return lax.dot_general(a, b, (((1,), (1,)), ((), ())), preferred_element_type=F32)


def _proj_kernel(x_ref, posc_ref, posr_ref, g_pre_ref, w_a_ref, w_bt_ref, g_q_ref, w_qup_ref,
                 g_kv_ref, w_v_ref, w_knt_ref, invf_lane_ref, invf_col_ref,
                 sbq_ref, sbkt_ref, sbv_ref, mq_ref, mkt_ref, mv_ref):
    h = _rms(x_ref[0], g_pre_ref[...]).astype(BF16)
    pa = _dot(h, w_a_ref[...])
    sbq_ref[0] = (pa[:, :SB_WIDTH] * SB_SCALE).astype(BF16)
    sbv_ref[0] = pa[:, SB_WIDTH:2 * SB_WIDTH].astype(BF16)
    pbt = _dot_nt(w_bt_ref[...], h)
    sbkt_ref[0] = pbt[:SB_WIDTH].astype(BF16)

    c0 = 2 * SB_WIDTH
    nq = _rms(pa[:, c0:c0 + Q_LORA_RANK], g_q_ref[...]).astype(BF16)
    q = _dot(nq, w_qup_ref[...])
    ang = posc_ref[0] * invf_lane_ref[...]
    cosq = jnp.cos(ang)
    sinq = jnp.sin(ang)
    lane = lax.broadcasted_iota(jnp.int32, ang.shape, 1)
    half = QK_ROPE_DIM // 2
    first = (lane >= QK_NOPE_DIM) & (lane < QK_NOPE_DIM + half)
    second = (lane >= QK_NOPE_DIM + half) & (lane < QK_NOPE_DIM + QK_ROPE_DIM)
    cfull = jnp.where(lane < QK_NOPE_DIM, 1.0, cosq)
    s_first = jnp.where(first, -sinq, 0.0)
    s_second = jnp.where(second, sinq, 0.0)
    for hd in range(MLA_HEADS):
        qh = q[:, hd * LANES:(hd + 1) * LANES]
        up = pltpu.roll(qh, LANES - half, 1)
        dn = pltpu.roll(qh, half, 1)
        mq_ref[0, :, hd * LANES:(hd + 1) * LANES] = (qh * cfull + up * s_first + dn * s_second).astype(BF16)

    c1 = c0 + Q_LORA_RANK
    nkv = _rms(pa[:, c1:c1 + KV_LORA_RANK], g_kv_ref[...]).astype(BF16)
    mv_ref[0] = _dot(nkv, w_v_ref[...]).astype(BF16)
    knt = _dot_nt(w_knt_ref[...], nkv)
    kr = pbt[SB_WIDTH:SB_WIDTH + QK_ROPE_DIM]
    angt = invf_col_ref[...] * posr_ref[0]
    cost = jnp.cos(angt)
    sint = jnp.sin(angt)
    x1 = kr[:half]
    x2 = kr[half:]
    krt = jnp.concatenate(
        [x1 * cost - x2 * sint, x2 * cost + x1 * sint,
         jnp.zeros((MLA_QK_PAD - QK_NOPE_DIM - QK_ROPE_DIM, kr.shape[1]), F32)], axis=0).astype(BF16)
    for hd in range(MLA_HEADS):
        r0 = hd * MLA_QK_PAD
        mkt_ref[0, r0:r0 + QK_NOPE_DIM, :] = knt[hd * QK_NOPE_DIM:(hd + 1) * QK_NOPE_DIM].astype(BF16)
        mkt_ref[0, r0 + QK_NOPE_DIM:r0 + MLA_QK_PAD, :] = krt


def _const_spec(shape):
    return pl.BlockSpec(shape, lambda *_: (0,) * len(shape))


def _proj_call(x, posc, posr, g_pre, w_a, w_bt, g_q, w_qup, g_kv, w_v, w_knt, invf_lane, invf_col):
    b, s, d = x.shape
    tm = PROJ_TM
    tok = lambda w: pl.BlockSpec((1, tm, w), lambda i, j: (i, j, 0))
    tok_t = lambda r: pl.BlockSpec((1, r, tm), lambda i, j: (i, 0, j))
    mqw = MLA_HEADS * MLA_QK_PAD
    return pl.pallas_call(
        _proj_kernel,
        grid=(b, s // tm),
        in_specs=[tok(d), tok(1), tok_t(1), _const_spec(g_pre.shape), _const_spec(w_a.shape),
                  _const_spec(w_bt.shape), _const_spec(g_q.shape), _const_spec(w_qup.shape),
                  _const_spec(g_kv.shape), _const_spec(w_v.shape), _const_spec(w_knt.shape),
                  _const_spec(invf_lane.shape), _const_spec(invf_col.shape)],
        out_specs=[tok(SB_WIDTH), tok_t(SB_WIDTH), tok(SB_WIDTH), tok(mqw), tok_t(mqw), tok(MLA_WIDTH)],
        out_shape=[jax.ShapeDtypeStruct((b, s, SB_WIDTH), BF16),
                   jax.ShapeDtypeStruct((b, SB_WIDTH, s), BF16),
                   jax.ShapeDtypeStruct((b, s, SB_WIDTH), BF16),
                   jax.ShapeDtypeStruct((b, s, mqw), BF16),
                   jax.ShapeDtypeStruct((b, mqw, s), BF16),
                   jax.ShapeDtypeStruct((b, s, MLA_WIDTH), BF16)],
        compiler_params=pltpu.CompilerParams(dimension_semantics=("arbitrary", "arbitrary"),
                                             vmem_limit_bytes=VMEM_LIMIT),
        name="proj",
    )(x, posc, posr, g_pre, w_a, w_bt, g_q, w_qup, g_kv, w_v, w_knt, invf_lane, invf_col)


def _softplus(z):
    return jnp.maximum(z, 0.0) + jnp.log1p(jnp.exp(-jnp.abs(z)))


def _sb_kernel(q_ref, kt_ref, v_ref, tri_ref, g_ref, o_ref):
    tq, tk = ATT_TQ, ATT_TK
    qb = pl.program_id(1)
    lane = lax.broadcasted_iota(jnp.int32, (1, LANES), 1)
    low = lane < SB_HEAD_DIM
    tri = tri_ref[...]
    strict = (lax.broadcasted_iota(jnp.int32, (tq, tk), 1)
              < lax.broadcasted_iota(jnp.int32, (tq, tk), 0))
    outs = []
    for pair in range(HEAD_PAIRS):
        cols = slice(pair * LANES, (pair + 1) * LANES)
        q128 = q_ref[0, :, cols]
        acc = jnp.zeros((tq, LANES), F32)
        for hh in range(2):
            msk = low if hh == 0 else jnp.logical_not(low)
            qm = jnp.where(msk, q128, jnp.zeros_like(q128))

            def tile(k0, carry, acc, diag, qm=qm, msk=msk, cols=cols):
                z = _dot(qm, kt_ref[0, cols, pl.ds(k0, tk)])
                sp = _softplus(z)
                if diag:
                    sp = jnp.where(strict, sp, 0.0)
                hi = sp.astype(BF16)
                lo = (sp - hi.astype(F32)).astype(BF16)
                c = _dot(hi, tri) + _dot(lo, tri) + carry
                w = jnp.exp(z - c)
                if diag:
                    w = jnp.where(strict, w, 0.0)
                v128 = v_ref[0, pl.ds(k0, tk), cols]
                vm = jnp.where(msk, v128, jnp.zeros_like(v128))
                acc = acc + _dot(w.astype(BF16), vm)
                return c[:, :1], acc

            carry, acc = tile(pl.multiple_of(qb * tk, tk), jnp.zeros((tq, 1), F32), acc, True)

            def body(i, ca, tile=tile):
                k0 = pl.multiple_of((qb - 1 - i) * tk, tk)
                return tile(k0, ca[0], ca[1], False)

            carry, acc = lax.fori_loop(0, qb, body, (carry, acc))
        outs.append(acc)
    o = jnp.concatenate(outs, axis=1)
    o_ref[0] = _rms(o, g_ref[...]).astype(BF16)


def _sb_call(sbq, sbkt, sbv, tri, g):
    b, s, w = sbq.shape
    tq = ATT_TQ
    return pl.pallas_call(
        _sb_kernel,
        grid=(b, s // tq),
        in_specs=[pl.BlockSpec((1, tq, w), lambda i, j: (i, j, 0)),
                  pl.BlockSpec((1, w, s), lambda i, j: (i, 0, 0)),
                  pl.BlockSpec((1, s, w), lambda i, j: (i, 0, 0)),
                  _const_spec(tri.shape), _const_spec(g.shape)],
        out_specs=pl.BlockSpec((1, tq, w), lambda i, j: (i, j, 0)),
        out_shape=jax.ShapeDtypeStruct((b, s, w), BF16),
        compiler_params=pltpu.CompilerParams(dimension_semantics=("arbitrary", "arbitrary"),
                                             vmem_limit_bytes=VMEM_LIMIT),
        name="sb_attn",
    )(sbq, sbkt, sbv, tri, g)


def _mla_kernel(q_ref, kt_ref, v_ref, g_ref, o_ref):
    tq, tk = ATT_TQ, ATT_TK
    qb = pl.program_id(1)
    lane = lax.broadcasted_iota(jnp.int32, (1, LANES), 1)
    low = lane < V_HEAD_DIM
    causal = (lax.broadcasted_iota(jnp.int32, (tq, tk), 1)
              <= lax.broadcasted_iota(jnp.int32, (tq, tk), 0))
    outs = []
    for pair in range(HEAD_PAIRS):
        vcols = slice(pair * LANES, (pair + 1) * LANES)
        o_pair = jnp.zeros((tq, LANES), F32)
        for hh in range(2):
            hd = 2 * pair + hh
            qcols = slice(hd * MLA_QK_PAD, (hd + 1) * MLA_QK_PAD)
            msk = low if hh == 0 else jnp.logical_not(low)
            qh = q_ref[0, :, qcols]

            def tile(k0, m, l, acc, diag, qh=qh, msk=msk, qcols=qcols, vcols=vcols):
                sc = _dot(qh, kt_ref[0, qcols, pl.ds(k0, tk)]) * MLA_SCALE
                if diag:
                    sc = jnp.where(causal, sc, -jnp.inf)
                m_new = jnp.maximum(m, jnp.max(sc, axis=-1, keepdims=True))
                p = jnp.exp(sc - m_new)
                alpha = jnp.exp(m - m_new)
                l = alpha * l + jnp.sum(p, axis=-1, keepdims=True)
                v128 = v_ref[0, pl.ds(k0, tk), vcols]
                vm = jnp.where(msk, v128, jnp.zeros_like(v128))
                acc = alpha * acc + _dot(p.astype(BF16), vm)
                return m_new, l, acc

            m, l, acc = tile(pl.multiple_of(qb * tk, tk), jnp.full((tq, 1), -jnp.inf, F32),
                             jnp.zeros((tq, 1), F32), jnp.zeros((tq, LANES), F32), True)

            def body(i, c, tile=tile):
                return tile(pl.multiple_of(i * tk, tk), c[0], c[1], c[2], False)

            m, l, acc = lax.fori_loop(0, qb, body, (m, l, acc))
            o_pair = o_pair + acc / l
        outs.append(o_pair)
    o = jnp.concatenate(outs, axis=1)
    o_ref[0] = _rms(o, g_ref[...]).astype(BF16)


def _mla_call(mq, mkt, mv, g):
    b, s, qw = mq.shape
    w = mv.shape[2]
    tq = ATT_TQ
    return pl.pallas_call(
        _mla_kernel,
        grid=(b, s // tq),
        in_specs=[pl.BlockSpec((1, tq, qw), lambda i, j: (i, j, 0)),
                  pl.BlockSpec((1, qw, s), lambda i, j: (i, 0, 0)),
                  pl.BlockSpec((1, s, w), lambda i, j: (i, 0, 0)),
                  _const_spec(g.shape)],
        out_specs=pl.BlockSpec((1, tq, w), lambda i, j: (i, j, 0)),
        out_shape=jax.ShapeDtypeStruct((b, s, w), BF16),
        compiler_params=pltpu.CompilerParams(dimension_semantics=("arbitrary", "arbitrary"),
                                             vmem_limit_bytes=VMEM_LIMIT),
        name="mla_attn",
    )(mq, mkt, mv, g)


def _mix_kernel(x_ref, nsb_ref, nmla_ref, w_o_ref, g_ref, o_ref):
    mix = _dot(nsb_ref[...], w_o_ref[:SB_WIDTH, :]) + _dot(nmla_ref[...], w_o_ref[SB_WIDTH:, :])
    o_ref[...] = x_ref[...] + _rms(mix, g_ref[...])


def _mix_call(x2d, nsb2d, nmla2d, w_o, g):
    n, d = x2d.shape
    tm = MIX_TM
    tok = lambda w: pl.BlockSpec((tm, w), lambda i: (i, 0))
    return pl.pallas_call(
        _mix_kernel,
        grid=(n // tm,),
        in_specs=[tok(d), tok(SB_WIDTH), tok(MLA_WIDTH), _const_spec(w_o.shape), _const_spec(g.shape)],
        out_specs=tok(d),
        out_shape=jax.ShapeDtypeStruct((n, d), F32),
        compiler_params=pltpu.CompilerParams(dimension_semantics=("arbitrary",),
                                             vmem_limit_bytes=VMEM_LIMIT),
        name="mix",
    )(x2d, nsb2d, nmla2d, w_o, g)


def _gelu_tanh(x):
    return 0.5 * x * (1.0 + jnp.tanh(0.7978845608028654 * (x + 0.044715 * (x * x * x))))


def _ffn_kernel(x_ref, p_ref, g_pre_ref, w_up_ref, cw_ref, cb_ref, w_down_ref, g_post_ref,
                w_ple_ref, g_gate_ref, w_gate_ref, b_gate_ref, g_ple_ref, o_ref,
                ubuf_ref, halo_ref, facc_ref):
    tm, ch, hl = FFN_TM, FFN_CH, CONV_HALO
    j = pl.program_id(1)

    @pl.when(j == 0)
    def _():
        halo_ref[...] = jnp.zeros_like(halo_ref)

    x = x_ref[0]
    h = _rms(x, g_pre_ref[...]).astype(BF16)
    for c in range(D_FF // ch):
        for part in range(2):
            c0 = part * D_FF + c * ch
            ubuf_ref[0:hl, part * ch:(part + 1) * ch] = halo_ref[c, :, part * ch:(part + 1) * ch]
            ubuf_ref[hl:hl + tm, part * ch:(part + 1) * ch] = _dot(h, w_up_ref[:, c0:c0 + ch])
        halo_ref[c] = ubuf_ref[tm:tm + hl, :]
        ys = []
        for part in range(2):
            c0 = part * D_FF + c * ch
            pc = slice(part * ch, (part + 1) * ch)
            y = cb_ref[:, c0:c0 + ch]
            for k in range(CONV_WIDTH):
                sh = hl - (CONV_WIDTH - 1) + k
                y = y + cw_ref[k:k + 1, c0:c0 + ch] * ubuf_ref[sh:sh + tm, pc]
            ys.append(y)
        act = (_gelu_tanh(ys[0]) * ys[1]).astype(BF16)
        contrib = _dot(act, w_down_ref[c * ch:(c + 1) * ch, :])
        if c == 0:
            facc_ref[...] = contrib
        else:
            facc_ref[...] += contrib
    x2 = x + _rms(facc_ref[...], g_post_ref[...])
    e = _dot(p_ref[0].astype(BF16), w_ple_ref[...])
    gl = _dot(_rms(x2, g_gate_ref[...]).astype(BF16), w_gate_ref[...]) + b_gate_ref[...]
    o_ref[0] = x2 + _rms(jax.nn.sigmoid(gl) * e, g_ple_ref[...])


def _ffn_call(x1, p, g_pre, w_up, cw, cb, w_down, g_post, w_ple, g_gate, w_gate, b_gate, g_ple):
    b, s, d = x1.shape
    tm = FFN_TM
    tok = lambda w: pl.BlockSpec((1, tm, w), lambda i, j: (i, j, 0))
    consts = [g_pre, w_up, cw, cb, w_down, g_post, w_ple, g_gate, w_gate, b_gate, g_ple]
    return pl.pallas_call(
        _ffn_kernel,
        grid=(b, s // tm),
        in_specs=[tok(d), tok(PLE_DIM)] + [
            pl.BlockSpec(a.shape, lambda *_: (0, 0), pipeline_mode=pl.Buffered(1)) for a in consts],
        out_specs=tok(d),
        out_shape=jax.ShapeDtypeStruct((b, s, d), F32),
        scratch_shapes=[pltpu.VMEM((tm + CONV_HALO, 2 * FFN_CH), F32),
                        pltpu.VMEM((D_FF // FFN_CH, CONV_HALO, 2 * FFN_CH), F32),
                        pltpu.VMEM((tm, d), F32)],
        compiler_params=pltpu.CompilerParams(dimension_semantics=("arbitrary", "arbitrary"),
                                             vmem_limit_bytes=VMEM_LIMIT),
        name="ffn_ple",
    )(x1, p, *consts)


def _layer(x, p_i, posc, posr, invf_lane, invf_col, tri, w_in, g_pre_mix, g_q_lat, w_q_up, g_kv_lat,
           w_kv_up, g_grp_sb, g_grp_mla, w_o, g_post_mix, g_pre_ffn, w_up, conv_w, conv_b, w_down,
           g_post_ffn, w_ple, g_ple_gate, w_ple_gate, b_ple_gate, g_post_ple):
    b, s, d = x.shape
    row = lambda v: v.reshape(1, -1)
    o_q, o_k, o_v = 0, SB_WIDTH, 2 * SB_WIDTH
    o_ql = 3 * SB_WIDTH
    o_kvl = o_ql + Q_LORA_RANK
    o_kr = o_kvl + KV_LORA_RANK
    w_a = jnp.concatenate([w_in[:, o_q:o_k], w_in[:, o_v:o_ql], w_in[:, o_ql:o_kr]], axis=1).astype(BF16)
    w_bt = jnp.concatenate([w_in[:, o_k:o_v], w_in[:, o_kr:]], axis=1).T.astype(BF16)
    qk = QK_NOPE_DIM + QK_ROPE_DIM
    w_qup = jnp.pad(w_q_up.reshape(Q_LORA_RANK, MLA_HEADS, qk),
                    ((0, 0), (0, 0), (0, MLA_QK_PAD - qk))).reshape(Q_LORA_RANK, -1).astype(BF16)
    w_kv3 = w_kv_up.reshape(KV_LORA_RANK, MLA_HEADS, QK_NOPE_DIM + V_HEAD_DIM)
    w_knt = w_kv3[:, :, :QK_NOPE_DIM].reshape(KV_LORA_RANK, -1).T.astype(BF16)
    w_v = w_kv3[:, :, QK_NOPE_DIM:].reshape(KV_LORA_RANK, -1).astype(BF16)

    sbq, sbkt, sbv, mq, mkt, mv = _proj_call(
        x, posc, posr, row(g_pre_mix), w_a, w_bt, row(g_q_lat), w_qup, row(g_kv_lat), w_v, w_knt,
        invf_lane, invf_col)
    nsb = _sb_call(sbq, sbkt, sbv, tri, row(g_grp_sb))
    nmla = _mla_call(mq, mkt, mv, row(g_grp_mla))
    x1 = _mix_call(x.reshape(b * s, d), nsb.reshape(b * s, -1), nmla.reshape(b * s, -1),
                   w_o.astype(BF16), row(g_post_mix)).reshape(b, s, d)
    return _ffn_call(x1, p_i, row(g_pre_ffn), w_up.astype(BF16), conv_w, row(conv_b),
                     w_down.astype(BF16), row(g_post_ffn), w_ple.astype(BF16), row(g_ple_gate),
                     w_ple_gate.astype(BF16), row(b_ple_gate), row(g_post_ple))


def kernel(x, p, positions, w_in, g_pre_mix, g_q_lat, w_q_up, g_kv_lat, w_kv_up, g_grp_sb, g_grp_mla, w_o, g_post_mix, g_pre_ffn, w_up, conv_w, conv_b, w_down, g_post_ffn, w_ple, g_ple_gate, w_ple_gate, b_ple_gate, g_post_ple):
    depth = p.shape[0]
    pos_f = positions.astype(F32)
    posc = pos_f[:, :, None]
    posr = pos_f[:, None, :]
    inv_freq = 1.0 / (ROPE_THETA ** (jnp.arange(0, QK_ROPE_DIM, 2, dtype=F32) / QK_ROPE_DIM))
    invf_lane = jnp.concatenate([jnp.zeros((QK_NOPE_DIM,), F32), inv_freq, inv_freq,
                                 jnp.zeros((MLA_QK_PAD - QK_NOPE_DIM - QK_ROPE_DIM,), F32)]).reshape(1, -1)
    invf_col = inv_freq.reshape(-1, 1)
    kk = jnp.arange(ATT_TK)
    tri = (kk[:, None] >= kk[None, :]).astype(BF16)
    params = (w_in, g_pre_mix, g_q_lat, w_q_up, g_kv_lat, w_kv_up, g_grp_sb, g_grp_mla, w_o, g_post_mix,
              g_pre_ffn, w_up, conv_w, conv_b, w_down, g_post_ffn, w_ple, g_ple_gate, w_ple_gate,
              b_ple_gate, g_post_ple)
    for i in range(depth):
        x = _layer(x, p[i], posc, posr, invf_lane, invf_col, tri, *[w[i] for w in params])
    return x
```

```python
import jax
import jax.numpy as jnp
from jax import lax
from jax.experimental import pallas as pl
from jax.experimental.pallas import tpu as pltpu

F32 = jnp.float32
BF16 = jnp.bfloat16

D_MODEL = 1024
SB_HEADS = 8
SB_HEAD_DIM = 64
MLA_HEADS = 8
QK_NOPE_DIM = 64
QK_ROPE_DIM = 32
V_HEAD_DIM = 64
Q_LORA_RANK = 256
KV_LORA_RANK = 128
SB_WIDTH = SB_HEADS * SB_HEAD_DIM
MLA_WIDTH = MLA_HEADS * V_HEAD_DIM
D_FF = 2816
CONV_WIDTH = 3
PLE_DIM = 256
ROPE_THETA = 10000.0
NORM_EPS = 1e-6

LANES = 128
MLA_QK_PAD = LANES
MLA_QK_WIDTH = MLA_HEADS * MLA_QK_PAD
ROPE_HALF = QK_ROPE_DIM // 2
SB_SCALE = SB_HEAD_DIM ** -0.5
MLA_SCALE_LOG2E = (QK_NOPE_DIM + QK_ROPE_DIM) ** -0.5 * 1.4426950408889634

PROJ_TM = 512
ATT_TQ = 256
ATT_TK = 256
MIX_TM = 512
FFN_TM = 512
FFN_CH = 256
CONV_HALO = 8
VMEM_LIMIT = 56 * 1024 * 1024


def _rms(x, g):
    ms = jnp.mean(x * x, axis=-1, keepdims=True)
    return x * lax.rsqrt(ms + NORM_EPS) * g


def _dot(a, b):
    return jnp.dot(a, b, preferred_element_type=F32)


def _dot_nt(a, b):
    return lax.dot_general(a, b, (((1,), (1,)), ((), ())), preferred_element_type=F32)


def _proj_kernel(x_ref, posc_ref, posr_ref, g_pre_ref, w_a_ref, w_bt_ref, g_q_ref, w_qupt_ref,
                 g_kv_ref, w_kn_ref, w_vt_ref, invf_lane_ref, invf_col_ref,
                 sbqt_ref, sbk_ref, sbvt_ref, mqt_ref, mk_ref, mvt_ref):
    h = _rms(x_ref[0], g_pre_ref[...]).astype(BF16)
    pa = _dot(h, w_a_ref[...])
    sbk_ref[0] = pa[:, :SB_WIDTH].astype(BF16)
    pbt = _dot_nt(w_bt_ref[...], h)
    sbqt_ref[0] = (pbt[:SB_WIDTH] * SB_SCALE).astype(BF16)
    sbvt_ref[0] = pbt[SB_WIDTH:].astype(BF16)

    c0 = SB_WIDTH
    nq = _rms(pa[:, c0:c0 + Q_LORA_RANK], g_q_ref[...]).astype(BF16)
    qt = _dot_nt(w_qupt_ref[...], nq)
    angt = invf_col_ref[...] * posr_ref[0]
    cost = jnp.cos(angt)
    sint = jnp.sin(angt)
    for hd in range(MLA_HEADS):
        r0 = hd * MLA_QK_PAD
        r1 = r0 + QK_NOPE_DIM
        x1 = qt[r1:r1 + ROPE_HALF]
        x2 = qt[r1 + ROPE_HALF:r1 + QK_ROPE_DIM]
        mqt_ref[0, r0:r1, :] = qt[r0:r1].astype(BF16)
        mqt_ref[0, r1:r1 + ROPE_HALF, :] = (x1 * cost - x2 * sint).astype(BF16)
        mqt_ref[0, r1 + ROPE_HALF:r1 + QK_ROPE_DIM, :] = (x2 * cost + x1 * sint).astype(BF16)
        mqt_ref[0, r1 + QK_ROPE_DIM:r0 + MLA_QK_PAD, :] = qt[r1 + QK_ROPE_DIM:r0 + MLA_QK_PAD].astype(BF16)

    c1 = c0 + Q_LORA_RANK
    nkv = _rms(pa[:, c1:c1 + KV_LORA_RANK], g_kv_ref[...]).astype(BF16)
    mvt_ref[0] = _dot_nt(w_vt_ref[...], nkv).astype(BF16)
    kn = _dot(nkv, w_kn_ref[...])
    kr = pa[:, c1 + KV_LORA_RANK:c1 + KV_LORA_RANK + LANES]
    ang = posc_ref[0] * invf_lane_ref[...]
    cosq = jnp.cos(ang)
    sinq = jnp.sin(ang)
    lane = lax.broadcasted_iota(jnp.int32, ang.shape, 1)
    first = (lane >= QK_NOPE_DIM) & (lane < QK_NOPE_DIM + ROPE_HALF)
    second = (lane >= QK_NOPE_DIM + ROPE_HALF) & (lane < QK_NOPE_DIM + QK_ROPE_DIM)
    up = pltpu.roll(kr, LANES - ROPE_HALF, 1)
    dn = pltpu.roll(kr, ROPE_HALF, 1)
    krr = kr * cosq + up * jnp.where(first, -sinq, 0.0) + dn * jnp.where(second, sinq, 0.0)
    for hd in range(MLA_HEADS):
        cols = slice(hd * MLA_QK_PAD, (hd + 1) * MLA_QK_PAD)
        mk_ref[0, :, cols] = (kn[:, cols] + krr).astype(BF16)


def _const_spec(shape):
    return pl.BlockSpec(shape, lambda *_: (0,) * len(shape))


def _proj_call(x, posc, posr, g_pre, w_a, w_bt, g_q, w_qupt, g_kv, w_kn, w_vt, invf_lane, invf_col):
    b, s, d = x.shape
    tm = PROJ_TM
    tok = lambda w: pl.BlockSpec((1, tm, w), lambda i, j: (i, j, 0))
    tok_t = lambda r: pl.BlockSpec((1, r, tm), lambda i, j: (i, 0, j))
    return pl.pallas_call(
        _proj_kernel,
        grid=(b, s // tm),
        in_specs=[tok(d), tok(1), tok_t(1), _const_spec(g_pre.shape), _const_spec(w_a.shape),
                  _const_spec(w_bt.shape), _const_spec(g_q.shape), _const_spec(w_qupt.shape),
                  _const_spec(g_kv.shape), _const_spec(w_kn.shape), _const_spec(w_vt.shape),
                  _const_spec(invf_lane.shape), _const_spec(invf_col.shape)],
        out_specs=[tok_t(SB_WIDTH), tok(SB_WIDTH), tok_t(SB_WIDTH),
                   tok_t(MLA_QK_WIDTH), tok(MLA_QK_WIDTH), tok_t(MLA_WIDTH)],
        out_shape=[jax.ShapeDtypeStruct((b, SB_WIDTH, s), BF16),
                   jax.ShapeDtypeStruct((b, s, SB_WIDTH), BF16),
                   jax.ShapeDtypeStruct((b, SB_WIDTH, s), BF16),
                   jax.ShapeDtypeStruct((b, MLA_QK_WIDTH, s), BF16),
                   jax.ShapeDtypeStruct((b, s, MLA_QK_WIDTH), BF16),
                   jax.ShapeDtypeStruct((b, MLA_WIDTH, s), BF16)],
        compiler_params=pltpu.CompilerParams(dimension_semantics=("arbitrary", "arbitrary"),
                                             vmem_limit_bytes=VMEM_LIMIT),
        name="proj",
    )(x, posc, posr, g_pre, w_a, w_bt, g_q, w_qupt, g_kv, w_kn, w_vt, invf_lane, invf_col)


def _staged(items):
    nsteps = max(i + delays[-1] for i, (_, delays) in enumerate(items)) + 1
    for step in range(nsteps):
        for i, (fns, delays) in enumerate(items):
            for s in reversed(range(len(fns))):
                if step - delays[s] == i:
                    fns[s]()


def _group_norm(acc_ref, g):
    ot = jnp.concatenate([acc_ref[hd] for hd in range(acc_ref.shape[0])], axis=0)
    return _rms(ot.T, g).astype(BF16)


def _softplus(z):
    neg_abs = pltpu.bitcast(pltpu.bitcast(z, jnp.uint32) | jnp.uint32(0x80000000), F32)
    return jnp.maximum(z, 0.0) + jnp.log(1.0 + jnp.exp(neg_abs))


def _attn_kernel(sqt_ref, sk_ref, svt_ref, tri_ref, gs_ref, mqt_ref, mk_ref, mvt_ref, gm_ref, o_ref,
                 qw_ref, sacc_ref, macc_ref):
    tq, tk = ATT_TQ, ATT_TK
    qb = pl.program_id(1)
    row = lax.broadcasted_iota(jnp.int32, (LANES, tq), 0)
    for hd in range(SB_HEADS):
        pair = hd // 2
        blk = sqt_ref[0, pair * LANES:(pair + 1) * LANES, :]
        keep = (row < SB_HEAD_DIM) if hd % 2 == 0 else (row >= SB_HEAD_DIM)
        qw_ref[hd] = jnp.where(keep, blk, jnp.zeros_like(blk))
    tri = tri_ref[...]
    key_i = lax.broadcasted_iota(jnp.int32, (tk, tq), 0)
    qry_i = lax.broadcasted_iota(jnp.int32, (tk, tq), 1)
    strict = key_i < qry_i
    causal = key_i <= qry_i

    def tile(k0, state, diag):
        carries, stats = state if state is not None else (None, None)
        zts, cts, sts, new_c, new_s = {}, {}, {}, {}, {}

        def sb_item(hd):
            def logits():
                pair = hd // 2
                kp = sk_ref[0, pl.ds(k0, tk), pair * LANES:(pair + 1) * LANES]
                zts[hd] = _dot(kp, qw_ref[hd])

            def cumsum():
                sp = _softplus(zts[hd])
                if diag:
                    sp = jnp.where(strict, sp, 0.0)
                hi = sp.astype(BF16)
                lo = (sp - hi.astype(F32)).astype(BF16)
                cts[hd] = _dot(tri, hi) + _dot(tri, lo)

            def weigh():
                ct = cts.pop(hd)
                wt = jnp.exp(zts.pop(hd) - ct)
                if diag:
                    wt = jnp.where(strict, wt, 0.0)
                contrib = _dot(svt_ref[0, hd * SB_HEAD_DIM:(hd + 1) * SB_HEAD_DIM, pl.ds(k0, tk)],
                               wt.astype(BF16))
                if diag:
                    sacc_ref[hd] = contrib
                    new_c[hd] = ct[0:1, :]
                else:
                    sacc_ref[hd] += contrib * jnp.exp(-carries[hd])
                    new_c[hd] = ct[0:1, :] + carries[hd]

            return (logits, cumsum, weigh), (0, 3, 6)

        def mla_item(hd):
            def scores():
                cols = slice(hd * MLA_QK_PAD, (hd + 1) * MLA_QK_PAD)
                st = _dot(mk_ref[0, pl.ds(k0, tk), cols], mqt_ref[0, cols, :])
                sts[hd] = jnp.where(causal, st, -jnp.inf) if diag else st

            def weigh():
                st = sts.pop(hd)
                mt = jnp.max(st, axis=0, keepdims=True)
                m_new = mt if diag else jnp.maximum(stats[hd][0], mt)
                p = jnp.exp2((st - m_new) * MLA_SCALE_LOG2E)
                ps = jnp.sum(p, axis=0, keepdims=True)
                contrib = _dot(mvt_ref[0, hd * V_HEAD_DIM:(hd + 1) * V_HEAD_DIM, pl.ds(k0, tk)], p.astype(BF16))
                if diag:
                    macc_ref[hd] = contrib
                    l_new = ps
                else:
                    alpha = jnp.exp2((stats[hd][0] - m_new) * MLA_SCALE_LOG2E)
                    macc_ref[hd] = alpha * macc_ref[hd] + contrib
                    l_new = alpha * stats[hd][1] + ps
                new_s[hd] = (m_new, l_new)

            return (scores, weigh), (0, 3)

        items = []
        for hd in range(SB_HEADS):
            items.append(sb_item(hd))
            items.append(mla_item(hd))
        _staged(items)
        return (tuple(new_c[hd] for hd in range(SB_HEADS)), tuple(new_s[hd] for hd in range(MLA_HEADS)))

    state = tile(pl.multiple_of(qb * tk, tk), None, True)

    def body(i, state):
        return tile(pl.multiple_of((qb - 1 - i) * tk, tk), state, False)

    _, stats = lax.fori_loop(0, qb, body, state)
    for hd in range(MLA_HEADS):
        macc_ref[hd] = macc_ref[hd] / stats[hd][1]
    o_ref[0, :, :SB_WIDTH] = _group_norm(sacc_ref, gs_ref[...])
    o_ref[0, :, SB_WIDTH:] = _group_norm(macc_ref, gm_ref[...])


def _attn_call(sbqt, sbk, sbvt, tri, g_sb, mqt, mk, mvt, g_mla):
    b, s, w = sbk.shape
    qw = mk.shape[2]
    tq = ATT_TQ
    qcol = lambda r: pl.BlockSpec((1, r, tq), lambda i, j: (i, 0, j))
    seq = lambda r, c: pl.BlockSpec((1, r, c), lambda i, j: (i, 0, 0))
    return pl.pallas_call(
        _attn_kernel,
        grid=(b, s // tq),
        in_specs=[qcol(w), seq(s, w), seq(w, s), _const_spec(tri.shape), _const_spec(g_sb.shape),
                  qcol(qw), seq(s, qw), seq(w, s), _const_spec(g_mla.shape)],
        out_specs=pl.BlockSpec((1, tq, 2 * w), lambda i, j: (i, j, 0)),
        out_shape=jax.ShapeDtypeStruct((b, s, 2 * w), BF16),
        scratch_shapes=[pltpu.VMEM((SB_HEADS, LANES, tq), BF16),
                        pltpu.VMEM((SB_HEADS, SB_HEAD_DIM, tq), F32),
                        pltpu.VMEM((MLA_HEADS, V_HEAD_DIM, tq), F32)],
        compiler_params=pltpu.CompilerParams(dimension_semantics=("arbitrary", "arbitrary"),
                                             vmem_limit_bytes=VMEM_LIMIT),
        name="attn",
    )(sbqt, sbk, sbvt, tri, g_sb, mqt, mk, mvt, g_mla)


def _mix_kernel(x_ref, n_ref, w_o_ref, g_ref, o_ref):
    o_ref[...] = x_ref[...] + _rms(_dot(n_ref[...], w_o_ref[...]), g_ref[...])


def _mix_call(x2d, n2d, w_o, g):
    n, d = x2d.shape
    tm = MIX_TM
    tok = lambda w: pl.BlockSpec((tm, w), lambda i: (i, 0))
    return pl.pallas_call(
        _mix_kernel,
        grid=(n // tm,),
        in_specs=[tok(d), tok(n2d.shape[1]), _const_spec(w_o.shape), _const_spec(g.shape)],
        out_specs=tok(d),
        out_shape=jax.ShapeDtypeStruct((n, d), F32),
        compiler_params=pltpu.CompilerParams(dimension_semantics=("arbitrary",),
                                             vmem_limit_bytes=VMEM_LIMIT),
        name="mix",
    )(x2d, n2d, w_o, g)


def _gelu_tanh(x):
    return 0.5 * x * (1.0 + jnp.tanh(0.7978845608028654 * (x + 0.044715 * (x * x * x))))


def _ffn_kernel(x_ref, p_ref, g_pre_ref, w_up_ref, cw_ref, cb_ref, w_down_ref, g_post_ref,
                w_ple_ref, g_gate_ref, w_gate_ref, b_gate_ref, g_ple_ref, o_ref,
                ubuf_ref, halo_ref, facc_ref):
    tm, ch, hl = FFN_TM, FFN_CH, CONV_HALO
    j = pl.program_id(1)

    @pl.when(j == 0)
    def _():
        halo_ref[...] = jnp.zeros_like(halo_ref)

    x = x_ref[0]
    h = _rms(x, g_pre_ref[...]).astype(BF16)
    for c in range(D_FF // ch):
        for part in range(2):
            c0 = part * D_FF + c * ch
            ubuf_ref[0:hl, part * ch:(part + 1) * ch] = halo_ref[c, :, part * ch:(part + 1) * ch]
            ubuf_ref[hl:hl + tm, part * ch:(part + 1) * ch] = _dot(h, w_up_ref[:, c0:c0 + ch])
        halo_ref[c] = ubuf_ref[tm:tm + hl, :]
        ys = []
        for part in range(2):
            c0 = part * D_FF + c * ch
            pc = slice(part * ch, (part + 1) * ch)
            y = cb_ref[:, c0:c0 + ch]
            for k in range(CONV_WIDTH):
                sh = hl - (CONV_WIDTH - 1) + k
                y = y + cw_ref[k:k + 1, c0:c0 + ch] * ubuf_ref[sh:sh + tm, pc]
            ys.append(y)
        act = (_gelu_tanh(ys[0]) * ys[1]).astype(BF16)
        contrib = _dot(act, w_down_ref[c * ch:(c + 1) * ch, :])
        if c == 0:
            facc_ref[...] = contrib
        else:
            facc_ref[...] += contrib
    x2 = x + _rms(facc_ref[...], g_post_ref[...])
    e = _dot(p_ref[0].astype(BF16), w_ple_ref[...])
    gl = _dot(_rms(x2, g_gate_ref[...]).astype(BF16), w_gate_ref[...]) + b_gate_ref[...]
    o_ref[0] = x2 + _rms(jax.nn.sigmoid(gl) * e, g_ple_ref[...])


def _ffn_call(x1, p, g_pre, w_up, cw, cb, w_down, g_post, w_ple, g_gate, w_gate, b_gate, g_ple):
    b, s, d = x1.shape
    tm = FFN_TM
    tok = lambda w: pl.BlockSpec((1, tm, w), lambda i, j: (i, j, 0))
    consts = [g_pre, w_up, cw, cb, w_down, g_post, w_ple, g_gate, w_gate, b_gate, g_ple]
    return pl.pallas_call(
        _ffn_kernel,
        grid=(b, s // tm),
        in_specs=[tok(d), tok(PLE_DIM)] + [
            pl.BlockSpec(a.shape, lambda *_: (0, 0), pipeline_mode=pl.Buffered(1)) for a in consts],
        out_specs=tok(d),
        out_shape=jax.ShapeDtypeStruct((b, s, d), F32),
        scratch_shapes=[pltpu.VMEM((tm + CONV_HALO, 2 * FFN_CH), F32),
                        pltpu.VMEM((D_FF // FFN_CH, CONV_HALO, 2 * FFN_CH), F32),
                        pltpu.VMEM((tm, d), F32)],
        compiler_params=pltpu.CompilerParams(dimension_semantics=("arbitrary", "arbitrary"),
                                             vmem_limit_bytes=VMEM_LIMIT),
        name="ffn_ple",
    )(x1, p, *consts)


def _layer(x, p_i, posc, posr, invf_lane, invf_col, tri, w_in, g_pre_mix, g_q_lat, w_q_up, g_kv_lat,
           w_kv_up, g_grp_sb, g_grp_mla, w_o, g_post_mix, g_pre_ffn, w_up, conv_w, conv_b, w_down,
           g_post_ffn, w_ple, g_ple_gate, w_ple_gate, b_ple_gate, g_post_ple):
    b, s, d = x.shape
    row = lambda v: v.reshape(1, -1)
    o_k, o_v = SB_WIDTH, 2 * SB_WIDTH
    o_ql = 3 * SB_WIDTH
    o_kr = o_ql + Q_LORA_RANK + KV_LORA_RANK
    w_kr_pad = jnp.pad(w_in[:, o_kr:], ((0, 0), (QK_NOPE_DIM, MLA_QK_PAD - QK_NOPE_DIM - QK_ROPE_DIM)))
    w_a = jnp.concatenate([w_in[:, o_k:o_v], w_in[:, o_ql:o_kr], w_kr_pad], axis=1).astype(BF16)
    w_bt = jnp.concatenate([w_in[:, :o_k], w_in[:, o_v:o_ql]], axis=1).T.astype(BF16)
    qk = QK_NOPE_DIM + QK_ROPE_DIM
    w_qupt = jnp.pad(w_q_up.reshape(Q_LORA_RANK, MLA_HEADS, qk),
                     ((0, 0), (0, 0), (0, MLA_QK_PAD - qk))).reshape(Q_LORA_RANK, -1).T.astype(BF16)
    w_kv3 = w_kv_up.reshape(KV_LORA_RANK, MLA_HEADS, QK_NOPE_DIM + V_HEAD_DIM)
    w_kn = jnp.pad(w_kv3[:, :, :QK_NOPE_DIM],
                   ((0, 0), (0, 0), (0, MLA_QK_PAD - QK_NOPE_DIM))).reshape(KV_LORA_RANK, -1).astype(BF16)
    w_vt = w_kv3[:, :, QK_NOPE_DIM:].reshape(KV_LORA_RANK, -1).T.astype(BF16)

    sbqt, sbk, sbvt, mqt, mk, mvt = _proj_call(
        x, posc, posr, row(g_pre_mix), w_a, w_bt, row(g_q_lat), w_qupt, row(g_kv_lat), w_kn, w_vt,
        invf_lane, invf_col)
    n = _attn_call(sbqt, sbk, sbvt, tri, row(g_grp_sb), mqt, mk, mvt, row(g_grp_mla))
    x1 = _mix_call(x.reshape(b * s, d), n.reshape(b * s, -1), w_o.astype(BF16),
                   row(g_post_mix)).reshape(b, s, d)
    return _ffn_call(x1, p_i, row(g_pre_ffn), w_up.astype(BF16), conv_w, row(conv_b),
                     w_down.astype(BF16), row(g_post_ffn), w_ple.astype(BF16), row(g_ple_gate),
                     w_ple_gate.astype(BF16), row(b_ple_gate), row(g_post_ple))


def kernel(x, p, positions, w_in, g_pre_mix, g_q_lat, w_q_up, g_kv_lat, w_kv_up, g_grp_sb, g_grp_mla, w_o, g_post_mix, g_pre_ffn, w_up, conv_w, conv_b, w_down, g_post_ffn, w_ple, g_ple_gate, w_ple_gate, b_ple_gate, g_post_ple):
    depth = p.shape[0]
    pos_f = positions.astype(F32)
    posc = pos_f[:, :, None]
    posr = pos_f[:, None, :]
    inv_freq = 1.0 / (ROPE_THETA ** (jnp.arange(0, QK_ROPE_DIM, 2, dtype=F32) / QK_ROPE_DIM))
    invf_lane = jnp.concatenate([jnp.zeros((QK_NOPE_DIM,), F32), inv_freq, inv_freq,
                                 jnp.zeros((MLA_QK_PAD - QK_NOPE_DIM - QK_ROPE_DIM,), F32)]).reshape(1, -1)
    invf_col = inv_freq.reshape(-1, 1)
    kk = jnp.arange(ATT_TK)
    tri = (kk[None, :] >= kk[:, None]).astype(BF16)
    params = (w_in, g_pre_mix, g_q_lat, w_q_up, g_kv_lat, w_kv_up, g_grp_sb, g_grp_mla, w_o, g_post_mix,
              g_pre_ffn, w_up, conv_w, conv_b, w_down, g_post_ffn, w_ple, g_ple_gate, w_ple_gate,
              b_ple_gate, g_post_ple)
    for i in range(depth):
        x = _layer(x, p[i], posc, posr, invf_lane, invf_col, tri, *[w[i] for w in params])
    return x
```

```python
import jax
import jax.numpy as jnp
from jax import lax
from jax.experimental import pallas as pl
from jax.experimental.pallas import tpu as pltpu

F32 = jnp.float32
BF16 = jnp.bfloat16

D_MODEL = 1024
SB_HEADS = 8
SB_HEAD_DIM = 64
MLA_HEADS = 8
QK_NOPE_DIM = 64
QK_ROPE_DIM = 32
V_HEAD_DIM = 64
Q_LORA_RANK = 256
KV_LORA_RANK = 128
SB_WIDTH = SB_HEADS * SB_HEAD_DIM
MLA_WIDTH = MLA_HEADS * V_HEAD_DIM
D_FF = 2816
CONV_WIDTH = 3
PLE_DIM = 256
ROPE_THETA = 10000.0
NORM_EPS = 1e-6

LANES = 128
MLA_QK_PAD = LANES
MLA_QK_WIDTH = MLA_HEADS * MLA_QK_PAD
ROPE_HALF = QK_ROPE_DIM // 2
SB_SCALE = SB_HEAD_DIM ** -0.5
MLA_SCALE_LOG2E = (QK_NOPE_DIM + QK_ROPE_DIM) ** -0.5 * 1.4426950408889634

PROJ_TM = 512
ATT_TQ = 256
ATT_TK = 256
FFN_TM = 512
FFN_CH = 256
CONV_HALO = 8
FFN_RING = 7
VMEM_LIMIT = 56 * 1024 * 1024


def _rms(x, g):
    ms = jnp.mean(x * x, axis=-1, keepdims=True)
    return x * lax.rsqrt(ms + NORM_EPS) * g


def _dot(a, b):
    return jnp.dot(a, b, preferred_element_type=F32)


def _dot_nt(a, b):
    return lax.dot_general(a, b, (((1,), (1,)), ((), ())), preferred_element_type=F32)


def _proj_kernel(x_ref, posr_ref, g_pre_ref, w_a_ref, w_bt_ref, g_q_ref, w_qupt_ref,
                 g_kv_ref, w_kn_ref, w_vt_ref, invf_col_ref,
                 sbqt_ref, sbk_ref, sbvt_ref, mqt_ref, mk_ref, mvt_ref):
    h = _rms(x_ref[0], g_pre_ref[...]).astype(BF16)
    pa = _dot(h, w_a_ref[...])
    sbk_ref[0] = pa[:, :SB_WIDTH].astype(BF16)
    pbt = _dot_nt(w_bt_ref[...], h)
    sbqt_ref[0] = (pbt[:SB_WIDTH] * SB_SCALE).astype(BF16)
    sbvt_ref[0] = pbt[SB_WIDTH:2 * SB_WIDTH].astype(BF16)

    angt = invf_col_ref[...] * posr_ref[0]
    cost = jnp.cos(angt)
    sint = jnp.sin(angt)

    def rope_rows(t, r1):
        x1 = t[r1:r1 + ROPE_HALF]
        x2 = t[r1 + ROPE_HALF:r1 + QK_ROPE_DIM]
        return x1 * cost - x2 * sint, x2 * cost + x1 * sint

    c0 = SB_WIDTH
    nq = _rms(pa[:, c0:c0 + Q_LORA_RANK], g_q_ref[...]).astype(BF16)
    qt = _dot_nt(w_qupt_ref[...], nq)
    for hd in range(MLA_HEADS):
        r0 = hd * MLA_QK_PAD
        r1 = r0 + QK_NOPE_DIM
        y1, y2 = rope_rows(qt, r1)
        mqt_ref[0, r0:r1, :] = qt[r0:r1].astype(BF16)
        mqt_ref[0, r1:r1 + ROPE_HALF, :] = y1.astype(BF16)
        mqt_ref[0, r1 + ROPE_HALF:r1 + QK_ROPE_DIM, :] = y2.astype(BF16)
        mqt_ref[0, r1 + QK_ROPE_DIM:r0 + MLA_QK_PAD, :] = qt[r1 + QK_ROPE_DIM:r0 + MLA_QK_PAD].astype(BF16)

    c1 = c0 + Q_LORA_RANK
    nkv = _rms(pa[:, c1:c1 + KV_LORA_RANK], g_kv_ref[...]).astype(BF16)
    mvt_ref[0] = _dot_nt(w_vt_ref[...], nkv).astype(BF16)
    kn = _dot(nkv, w_kn_ref[...])
    krt = pbt[2 * SB_WIDTH:]
    y1, y2 = rope_rows(krt, QK_NOPE_DIM)
    krr = jnp.concatenate([krt[:QK_NOPE_DIM], y1, y2, krt[QK_NOPE_DIM + QK_ROPE_DIM:]], axis=0).T
    for hd in range(MLA_HEADS):
        cols = slice(hd * MLA_QK_PAD, (hd + 1) * MLA_QK_PAD)
        mk_ref[0, :, cols] = (kn[:, cols] + krr).astype(BF16)


def _const_spec(shape):
    return pl.BlockSpec(shape, lambda *_: (0,) * len(shape))


def _proj_call(x, posr, g_pre, w_a, w_bt, g_q, w_qupt, g_kv, w_kn, w_vt, invf_col):
    b, s, d = x.shape
    tm = PROJ_TM
    tok = lambda w: pl.BlockSpec((1, tm, w), lambda i, j: (i, j, 0))
    tok_t = lambda r: pl.BlockSpec((1, r, tm), lambda i, j: (i, 0, j))
    return pl.pallas_call(
        _proj_kernel,
        grid=(b, s // tm),
        in_specs=[tok(d), tok_t(1), _const_spec(g_pre.shape), _const_spec(w_a.shape),
                  _const_spec(w_bt.shape), _const_spec(g_q.shape), _const_spec(w_qupt.shape),
                  _const_spec(g_kv.shape), _const_spec(w_kn.shape), _const_spec(w_vt.shape),
                  _const_spec(invf_col.shape)],
        out_specs=[tok_t(SB_WIDTH), tok(SB_WIDTH), tok_t(SB_WIDTH),
                   tok_t(MLA_QK_WIDTH), tok(MLA_QK_WIDTH), tok_t(MLA_WIDTH)],
        out_shape=[jax.ShapeDtypeStruct((b, SB_WIDTH, s), BF16),
                   jax.ShapeDtypeStruct((b, s, SB_WIDTH), BF16),
                   jax.ShapeDtypeStruct((b, SB_WIDTH, s), BF16),
                   jax.ShapeDtypeStruct((b, MLA_QK_WIDTH, s), BF16),
                   jax.ShapeDtypeStruct((b, s, MLA_QK_WIDTH), BF16),
                   jax.ShapeDtypeStruct((b, MLA_WIDTH, s), BF16)],
        compiler_params=pltpu.CompilerParams(dimension_semantics=("arbitrary", "arbitrary"),
                                             vmem_limit_bytes=VMEM_LIMIT),
        name="proj",
    )(x, posr, g_pre, w_a, w_bt, g_q, w_qupt, g_kv, w_kn, w_vt, invf_col)


def _staged(items):
    nsteps = max(i + delays[-1] for i, (_, delays) in enumerate(items)) + 1
    for step in range(nsteps):
        for i, (fns, delays) in enumerate(items):
            for s in reversed(range(len(fns))):
                if step - delays[s] == i:
                    fns[s]()


def _group_norm(acc_ref, g):
    ot = jnp.concatenate([acc_ref[hd] for hd in range(acc_ref.shape[0])], axis=0)
    return _rms(ot.T, g).astype(BF16)


def _softplus(z):
    neg_abs = pltpu.bitcast(pltpu.bitcast(z, jnp.uint32) | jnp.uint32(0x80000000), F32)
    return jnp.maximum(z, 0.0) + jnp.log(1.0 + jnp.exp(neg_abs))


def _attn_kernel(sqt_ref, sk_ref, svt_ref, tri_ref, gs_ref, mqt_ref, mk_ref, mvt_ref, gm_ref, o_ref,
                 qw_ref, sacc_ref, macc_ref):
    tq, tk = ATT_TQ, ATT_TK
    qb = pl.program_id(1)
    row = lax.broadcasted_iota(jnp.int32, (LANES, tq), 0)
    for hd in range(SB_HEADS):
        pair = hd // 2
        blk = sqt_ref[0, pair * LANES:(pair + 1) * LANES, :]
        keep = (row < SB_HEAD_DIM) if hd % 2 == 0 else (row >= SB_HEAD_DIM)
        qw_ref[hd] = jnp.where(keep, blk, jnp.zeros_like(blk))
    tri = tri_ref[...]
    key_i = lax.broadcasted_iota(jnp.int32, (tk, tq), 0)
    qry_i = lax.broadcasted_iota(jnp.int32, (tk, tq), 1)
    strict = key_i < qry_i
    causal = key_i <= qry_i

    def tile(k0, state, diag):
        carries, stats = state if state is not None else (None, None)
        zts, cts, sts, new_c, new_s = {}, {}, {}, {}, {}

        def sb_item(hd):
            def logits():
                pair = hd // 2
                kp = sk_ref[0, pl.ds(k0, tk), pair * LANES:(pair + 1) * LANES]
                zts[hd] = _dot(kp, qw_ref[hd])

            def cumsum():
                sp = _softplus(zts[hd])
                if diag:
                    sp = jnp.where(strict, sp, 0.0)
                hi = sp.astype(BF16)
                lo = (sp - hi.astype(F32)).astype(BF16)
                cts[hd] = _dot(tri, hi) + _dot(tri, lo)

            def weigh():
                ct = cts.pop(hd)
                wt = jnp.exp(zts.pop(hd) - ct)
                if diag:
                    wt = jnp.where(strict, wt, 0.0)
                contrib = _dot(svt_ref[0, hd * SB_HEAD_DIM:(hd + 1) * SB_HEAD_DIM, pl.ds(k0, tk)],
                               wt.astype(BF16))
                if diag:
                    sacc_ref[hd] = contrib
                    new_c[hd] = ct[0:1, :]
                else:
                    sacc_ref[hd] += contrib * jnp.exp(-carries[hd])
                    new_c[hd] = ct[0:1, :] + carries[hd]

            return (logits, cumsum, weigh), (0, 3, 6)

        def mla_item(hd):
            def scores():
                cols = slice(hd * MLA_QK_PAD, (hd + 1) * MLA_QK_PAD)
                st = _dot(mk_ref[0, pl.ds(k0, tk), cols], mqt_ref[0, cols, :])
                sts[hd] = jnp.where(causal, st, -jnp.inf) if diag else st

            def weigh():
                st = sts.pop(hd)
                mt = jnp.max(st, axis=0, keepdims=True)
                m_new = mt if diag else jnp.maximum(stats[hd][0], mt)
                p = jnp.exp2((st - m_new) * MLA_SCALE_LOG2E)
                ps = jnp.sum(p, axis=0, keepdims=True)
                contrib = _dot(mvt_ref[0, hd * V_HEAD_DIM:(hd + 1) * V_HEAD_DIM, pl.ds(k0, tk)], p.astype(BF16))
                if diag:
                    macc_ref[hd] = contrib
                    l_new = ps
                else:
                    alpha = jnp.exp2((stats[hd][0] - m_new) * MLA_SCALE_LOG2E)
                    macc_ref[hd] = alpha * macc_ref[hd] + contrib
                    l_new = alpha * stats[hd][1] + ps
                new_s[hd] = (m_new, l_new)

            return (scores, weigh), (0, 3)

        items = []
        for hd in range(SB_HEADS):
            items.append(sb_item(hd))
            items.append(mla_item(hd))
        _staged(items)
        return (tuple(new_c[hd] for hd in range(SB_HEADS)), tuple(new_s[hd] for hd in range(MLA_HEADS)))

    state = tile(pl.multiple_of(qb * tk, tk), None, True)

    def body(i, state):
        return tile(pl.multiple_of((qb - 1 - i) * tk, tk), state, False)

    _, stats = lax.fori_loop(0, qb, body, state)
    for hd in range(MLA_HEADS):
        macc_ref[hd] = macc_ref[hd] / stats[hd][1]
    o_ref[0, :, :SB_WIDTH] = _group_norm(sacc_ref, gs_ref[...])
    o_ref[0, :, SB_WIDTH:] = _group_norm(macc_ref, gm_ref[...])


def _attn_call(sbqt, sbk, sbvt, tri, g_sb, mqt, mk, mvt, g_mla):
    b, s, w = sbk.shape
    qw = mk.shape[2]
    tq = ATT_TQ
    qcol = lambda r: pl.BlockSpec((1, r, tq), lambda i, j: (i, 0, j))
    seq = lambda r, c: pl.BlockSpec((1, r, c), lambda i, j: (i, 0, 0))
    return pl.pallas_call(
        _attn_kernel,
        grid=(b, s // tq),
        in_specs=[qcol(w), seq(s, w), seq(w, s), _const_spec(tri.shape), _const_spec(g_sb.shape),
                  qcol(qw), seq(s, qw), seq(w, s), _const_spec(g_mla.shape)],
        out_specs=pl.BlockSpec((1, tq, 2 * w), lambda i, j: (i, j, 0)),
        out_shape=jax.ShapeDtypeStruct((b, s, 2 * w), BF16),
        scratch_shapes=[pltpu.VMEM((SB_HEADS, LANES, tq), BF16),
                        pltpu.VMEM((SB_HEADS, SB_HEAD_DIM, tq), F32),
                        pltpu.VMEM((MLA_HEADS, V_HEAD_DIM, tq), F32)],
        compiler_params=pltpu.CompilerParams(dimension_semantics=("arbitrary", "arbitrary"),
                                             vmem_limit_bytes=VMEM_LIMIT),
        name="attn",
    )(sbqt, sbk, sbvt, tri, g_sb, mqt, mk, mvt, g_mla)


def _gelu_tanh(x):
    return 0.5 * x * (1.0 + jnp.tanh(0.7978845608028654 * (x + 0.044715 * (x * x * x))))


def _ffn_kernel(x_ref, n_ref, p_ref, w_o_ref, g_mix_ref, g_pre_ref, w_up_ref, cw_ref, cb_ref, w_down_ref,
                g_post_ref, w_ple_ref, g_gate_ref, w_gate_ref, b_gate_ref, g_ple_ref, o_ref,
                halo_ref, facc_ref, *ubuf_refs):
    tm, ch, hl = FFN_TM, FFN_CH, CONV_HALO
    j = pl.program_id(1)

    @pl.when(j == 0)
    def _():
        halo_ref[...] = jnp.zeros_like(halo_ref)

    o_ref[0] = x_ref[0] + _rms(_dot(n_ref[0], w_o_ref[...]), g_mix_ref[...])
    e = _dot(p_ref[0].astype(BF16), w_ple_ref[...])
    h = _rms(o_ref[0], g_pre_ref[...]).astype(BF16)

    def up(c):
        ubuf_ref = ubuf_refs[c % FFN_RING]
        for part in range(2):
            c0 = part * D_FF + c * ch
            pc = slice(part * ch, (part + 1) * ch)
            ubuf_ref[0:hl, pc] = halo_ref[c, :, pc]
            ubuf_ref[hl:hl + tm, pc] = _dot(h, w_up_ref[:, c0:c0 + ch])
        halo_ref[c] = ubuf_ref[tm:tm + hl, :]

    def down(c):
        ubuf_ref = ubuf_refs[c % FFN_RING]
        ys = []
        for part in range(2):
            c0 = part * D_FF + c * ch
            pc = slice(part * ch, (part + 1) * ch)
            y = cb_ref[:, c0:c0 + ch]
            for k in range(CONV_WIDTH):
                sh = hl - (CONV_WIDTH - 1) + k
                y = y + cw_ref[k:k + 1, c0:c0 + ch] * ubuf_ref[sh:sh + tm, pc]
            ys.append(y)
        act = (_gelu_tanh(ys[0]) * ys[1]).astype(BF16)
        contrib = _dot(act, w_down_ref[c * ch:(c + 1) * ch, :])
        if c == 0:
            facc_ref[...] = contrib
        else:
            facc_ref[...] += contrib

    _staged([((lambda c=c: up(c), lambda c=c: down(c)), (0, FFN_RING - 1)) for c in range(D_FF // ch)])

    x2 = o_ref[0] + _rms(facc_ref[...], g_post_ref[...])
    gl = _dot(_rms(x2, g_gate_ref[...]).astype(BF16), w_gate_ref[...]) + b_gate_ref[...]
    o_ref[0] = x2 + _rms(jax.nn.sigmoid(gl) * e, g_ple_ref[...])


def _ffn_call(x, n, p, w_o, g_mix, g_pre, w_up, cw, cb, w_down, g_post, w_ple, g_gate, w_gate, b_gate, g_ple):
    b, s, d = x.shape
    tm = FFN_TM
    tok = lambda w: pl.BlockSpec((1, tm, w), lambda i, j: (i, j, 0))
    consts = [w_o, g_mix, g_pre, w_up, cw, cb, w_down, g_post, w_ple, g_gate, w_gate, b_gate, g_ple]
    return pl.pallas_call(
        _ffn_kernel,
        grid=(b, s // tm),
        in_specs=[tok(d), tok(n.shape[2]), tok(PLE_DIM)] + [
            pl.BlockSpec(a.shape, lambda *_: (0, 0), pipeline_mode=pl.Buffered(1)) for a in consts],
        out_specs=tok(d),
        out_shape=jax.ShapeDtypeStruct((b, s, d), F32),
        scratch_shapes=[pltpu.VMEM((D_FF // FFN_CH, CONV_HALO, 2 * FFN_CH), F32),
                        pltpu.VMEM((tm, d), F32)]
        + [pltpu.VMEM((tm + CONV_HALO, 2 * FFN_CH), F32)] * FFN_RING,
        compiler_params=pltpu.CompilerParams(dimension_semantics=("arbitrary", "arbitrary"),
                                             vmem_limit_bytes=VMEM_LIMIT),
        name="ffn_ple",
    )(x, n, p, *consts)


def _layer(x, p_i, posr, invf_col, tri, w_in, g_pre_mix, g_q_lat, w_q_up, g_kv_lat,
           w_kv_up, g_grp_sb, g_grp_mla, w_o, g_post_mix, g_pre_ffn, w_up, conv_w, conv_b, w_down,
           g_post_ffn, w_ple, g_ple_gate, w_ple_gate, b_ple_gate, g_post_ple):
    b, s, d = x.shape
    row = lambda v: v.reshape(1, -1)
    o_k, o_v = SB_WIDTH, 2 * SB_WIDTH
    o_ql = 3 * SB_WIDTH
    o_kr = o_ql + Q_LORA_RANK + KV_LORA_RANK
    w_kr_pad = jnp.pad(w_in[:, o_kr:], ((0, 0), (QK_NOPE_DIM, MLA_QK_PAD - QK_NOPE_DIM - QK_ROPE_DIM)))
    w_a = jnp.concatenate([w_in[:, o_k:o_v], w_in[:, o_ql:o_kr]], axis=1).astype(BF16)
    w_bt = jnp.concatenate([w_in[:, :o_k], w_in[:, o_v:o_ql], w_kr_pad], axis=1).T.astype(BF16)
    qk = QK_NOPE_DIM + QK_ROPE_DIM
    w_qupt = jnp.pad(w_q_up.reshape(Q_LORA_RANK, MLA_HEADS, qk),
                     ((0, 0), (0, 0), (0, MLA_QK_PAD - qk))).reshape(Q_LORA_RANK, -1).T.astype(BF16)
    w_kv3 = w_kv_up.reshape(KV_LORA_RANK, MLA_HEADS, QK_NOPE_DIM + V_HEAD_DIM)
    w_kn = jnp.pad(w_kv3[:, :, :QK_NOPE_DIM],
                   ((0, 0), (0, 0), (0, MLA_QK_PAD - QK_NOPE_DIM))).reshape(KV_LORA_RANK, -1).astype(BF16)
    w_vt = w_kv3[:, :, QK_NOPE_DIM:].reshape(KV_LORA_RANK, -1).T.astype(BF16)

    sbqt, sbk, sbvt, mqt, mk, mvt = _proj_call(
        x, posr, row(g_pre_mix), w_a, w_bt, row(g_q_lat), w_qupt, row(g_kv_lat), w_kn, w_vt, invf_col)
    n = _attn_call(sbqt, sbk, sbvt, tri, row(g_grp_sb), mqt, mk, mvt, row(g_grp_mla))
    return _ffn_call(x, n, p_i, w_o.astype(BF16), row(g_post_mix), row(g_pre_ffn), w_up.astype(BF16), conv_w, row(conv_b),
                     w_down.astype(BF16), row(g_post_ffn), w_ple.astype(BF16), row(g_ple_gate),
                     w_ple_gate.astype(BF16), row(b_ple_gate), row(g_post_ple))


def kernel(x, p, positions, w_in, g_pre_mix, g_q_lat, w_q_up, g_kv_lat, w_kv_up, g_grp_sb, g_grp_mla, w_o, g_post_mix, g_pre_ffn, w_up, conv_w, conv_b, w_down, g_post_ffn, w_ple, g_ple_gate, w_ple_gate, b_ple_gate, g_post_ple):
    depth = p.shape[0]
    pos_f = positions.astype(F32)
    posr = pos_f[:, None, :]
    inv_freq = 1.0 / (ROPE_THETA ** (jnp.arange(0, QK_ROPE_DIM, 2, dtype=F32) / QK_ROPE_DIM))
    invf_col = inv_freq.reshape(-1, 1)
    kk = jnp.arange(ATT_TK)
    tri = (kk[None, :] >= kk[:, None]).astype(BF16)
    params = (w_in, g_pre_mix, g_q_lat, w_q_up, g_kv_lat, w_kv_up, g_grp_sb, g_grp_mla, w_o, g_post_mix,
              g_pre_ffn, w_up, conv_w, conv_b, w_down, g_post_ffn, w_ple, g_ple_gate, w_ple_gate,
              b_ple_gate, g_post_ple)
    for i in range(depth):
        x = _layer(x, p[i], posr, invf_col, tri, *[w[i] for w in params])
    return x
```

```python
import jax
import jax.numpy as jnp
from jax import lax
from jax.experimental import pallas as pl
from jax.experimental.pallas import tpu as pltpu

F32 = jnp.float32
BF16 = jnp.bfloat16

D_MODEL = 1024
SB_HEADS = 8
SB_HEAD_DIM = 64
MLA_HEADS = 8
QK_NOPE_DIM = 64
QK_ROPE_DIM = 32
V_HEAD_DIM = 64
Q_LORA_RANK = 256
KV_LORA_RANK = 128
SB_WIDTH = SB_HEADS * SB_HEAD_DIM
MLA_WIDTH = MLA_HEADS * V_HEAD_DIM
D_FF = 2816
CONV_WIDTH = 3
PLE_DIM = 256
ROPE_THETA = 10000.0
NORM_EPS = 1e-6

LANES = 128
MLA_QK_PAD = LANES
MLA_QK_WIDTH = MLA_HEADS * MLA_QK_PAD
ROPE_HALF = QK_ROPE_DIM // 2
SB_SCALE = SB_HEAD_DIM ** -0.5
MLA_SCALE_LOG2E = (QK_NOPE_DIM + QK_ROPE_DIM) ** -0.5 * 1.4426950408889634

PROJ_TM = 512
ATT_TQ = 256
ATT_TK = 256
FFN_TM = 512
FFN_CH = 256
CONV_HALO = 8
FFN_RING = 11
VMEM_LIMIT = 56 * 1024 * 1024


def _rms(x, g):
    ms = jnp.mean(x * x, axis=-1, keepdims=True)
    return x * lax.rsqrt(ms + NORM_EPS) * g


def _dot(a, b):
    return jnp.dot(a, b, preferred_element_type=F32)


def _dot_nt(a, b):
    return lax.dot_general(a, b, (((1,), (1,)), ((), ())), preferred_element_type=F32)


def _proj_kernel(x_ref, posr_ref, g_pre_ref, w_a_ref, w_bt_ref, g_q_ref, w_qupt_ref,
                 g_kv_ref, w_kn_ref, w_vt_ref, invf_col_ref,
                 sbqt_ref, sbk_ref, sbvt_ref, mqt_ref, mk_ref, mvt_ref):
    h = _rms(x_ref[0], g_pre_ref[...]).astype(BF16)
    pa = _dot(h, w_a_ref[...])
    sbk_ref[0] = pa[:, :SB_WIDTH].astype(BF16)
    pbt = _dot_nt(w_bt_ref[...], h)
    sbqt_ref[0] = (pbt[:SB_WIDTH] * SB_SCALE).astype(BF16)
    sbvt_ref[0] = pbt[SB_WIDTH:2 * SB_WIDTH].astype(BF16)

    angt = invf_col_ref[...] * posr_ref[0]
    cost = jnp.cos(angt)
    sint = jnp.sin(angt)

    def rope_rows(t, r1):
        x1 = t[r1:r1 + ROPE_HALF]
        x2 = t[r1 + ROPE_HALF:r1 + QK_ROPE_DIM]
        return x1 * cost - x2 * sint, x2 * cost + x1 * sint

    c0 = SB_WIDTH
    nq = _rms(pa[:, c0:c0 + Q_LORA_RANK], g_q_ref[...]).astype(BF16)
    qt = _dot_nt(w_qupt_ref[...], nq)
    for hd in range(MLA_HEADS):
        r0 = hd * MLA_QK_PAD
        r1 = r0 + QK_NOPE_DIM
        y1, y2 = rope_rows(qt, r1)
        mqt_ref[0, r0:r1, :] = qt[r0:r1].astype(BF16)
        mqt_ref[0, r1:r1 + ROPE_HALF, :] = y1.astype(BF16)
        mqt_ref[0, r1 + ROPE_HALF:r1 + QK_ROPE_DIM, :] = y2.astype(BF16)
        mqt_ref[0, r1 + QK_ROPE_DIM:r0 + MLA_QK_PAD, :] = qt[r1 + QK_ROPE_DIM:r0 + MLA_QK_PAD].astype(BF16)

    c1 = c0 + Q_LORA_RANK
    nkv = _rms(pa[:, c1:c1 + KV_LORA_RANK], g_kv_ref[...]).astype(BF16)
    mvt_ref[0] = _dot_nt(w_vt_ref[...], nkv).astype(BF16)
    kn = _dot(nkv, w_kn_ref[...])
    krt = pbt[2 * SB_WIDTH:]
    y1, y2 = rope_rows(krt, QK_NOPE_DIM)
    krr = jnp.concatenate([krt[:QK_NOPE_DIM], y1, y2, krt[QK_NOPE_DIM + QK_ROPE_DIM:]], axis=0).T
    for hd in range(MLA_HEADS):
        cols = slice(hd * MLA_QK_PAD, (hd + 1) * MLA_QK_PAD)
        mk_ref[0, :, cols] = (kn[:, cols] + krr).astype(BF16)


def _const_spec(shape):
    return pl.BlockSpec(shape, lambda *_: (0,) * len(shape))


def _proj_call(x, posr, g_pre, w_a, w_bt, g_q, w_qupt, g_kv, w_kn, w_vt, invf_col):
    b, s, d = x.shape
    tm = PROJ_TM
    tok = lambda w: pl.BlockSpec((1, tm, w), lambda i, j: (i, j, 0))
    tok_t = lambda r: pl.BlockSpec((1, r, tm), lambda i, j: (i, 0, j))
    return pl.pallas_call(
        _proj_kernel,
        grid=(b, s // tm),
        in_specs=[tok(d), tok_t(1), _const_spec(g_pre.shape), _const_spec(w_a.shape),
                  _const_spec(w_bt.shape), _const_spec(g_q.shape), _const_spec(w_qupt.shape),
                  _const_spec(g_kv.shape), _const_spec(w_kn.shape), _const_spec(w_vt.shape),
                  _const_spec(invf_col.shape)],
        out_specs=[tok_t(SB_WIDTH), tok(SB_WIDTH), tok_t(SB_WIDTH),
                   tok_t(MLA_QK_WIDTH), tok(MLA_QK_WIDTH), tok_t(MLA_WIDTH)],
        out_shape=[jax.ShapeDtypeStruct((b, SB_WIDTH, s), BF16),
                   jax.ShapeDtypeStruct((b, s, SB_WIDTH), BF16),
                   jax.ShapeDtypeStruct((b, SB_WIDTH, s), BF16),
                   jax.ShapeDtypeStruct((b, MLA_QK_WIDTH, s), BF16),
                   jax.ShapeDtypeStruct((b, s, MLA_QK_WIDTH), BF16),
                   jax.ShapeDtypeStruct((b, MLA_WIDTH, s), BF16)],
        compiler_params=pltpu.CompilerParams(dimension_semantics=("arbitrary", "arbitrary"),
                                             vmem_limit_bytes=VMEM_LIMIT),
        name="proj",
    )(x, posr, g_pre, w_a, w_bt, g_q, w_qupt, g_kv, w_kn, w_vt, invf_col)


def _staged(items):
    nsteps = max(i + delays[-1] for i, (_, delays) in enumerate(items)) + 1
    for step in range(nsteps):
        for i, (fns, delays) in enumerate(items):
            for s in reversed(range(len(fns))):
                if step - delays[s] == i:
                    fns[s]()


def _group_norm(acc_ref, g):
    ot = jnp.concatenate([acc_ref[hd] for hd in range(acc_ref.shape[0])], axis=0)
    return _rms(ot.T, g).astype(BF16)


def _softplus(z):
    neg_abs = pltpu.bitcast(pltpu.bitcast(z, jnp.uint32) | jnp.uint32(0x80000000), F32)
    return jnp.maximum(z, 0.0) + jnp.log(1.0 + jnp.exp(neg_abs))


def _attn_kernel(sqt_ref, sk_ref, svt_ref, tri_ref, gs_ref, mqt_ref, mk_ref, mvt_ref, gm_ref, o_ref,
                 qw_ref, sacc_ref, macc_ref):
    tq, tk = ATT_TQ, ATT_TK
    qb = pl.program_id(1)
    row = lax.broadcasted_iota(jnp.int32, (LANES, tq), 0)
    for hd in range(SB_HEADS):
        pair = hd // 2
        blk = sqt_ref[0, pair * LANES:(pair + 1) * LANES, :]
        keep = (row < SB_HEAD_DIM) if hd % 2 == 0 else (row >= SB_HEAD_DIM)
        qw_ref[hd] = jnp.where(keep, blk, jnp.zeros_like(blk))
    tri = tri_ref[...]
    key_i = lax.broadcasted_iota(jnp.int32, (tk, tq), 0)
    qry_i = lax.broadcasted_iota(jnp.int32, (tk, tq), 1)
    strict = key_i < qry_i
    causal = key_i <= qry_i

    def tile(k0, state, diag):
        carries, stats = state if state is not None else (None, None)
        zts, cts, sts, new_c, new_s = {}, {}, {}, {}, {}

        def sb_item(hd):
            def logits():
                pair = hd // 2
                kp = sk_ref[0, pl.ds(k0, tk), pair * LANES:(pair + 1) * LANES]
                zts[hd] = _dot(kp, qw_ref[hd])

            def cumsum():
                sp = _softplus(zts[hd])
                if diag:
                    sp = jnp.where(strict, sp, 0.0)
                cts[hd] = _dot(tri, sp.astype(BF16))

            def weigh():
                ct = cts.pop(hd)
                wt = jnp.exp(zts.pop(hd) - ct)
                if diag:
                    wt = jnp.where(strict, wt, 0.0)
                contrib = _dot(svt_ref[0, hd * SB_HEAD_DIM:(hd + 1) * SB_HEAD_DIM, pl.ds(k0, tk)],
                               wt.astype(BF16))
                if diag:
                    sacc_ref[hd] = contrib
                    new_c[hd] = ct[0:1, :]
                else:
                    sacc_ref[hd] += contrib * jnp.exp(-carries[hd])
                    new_c[hd] = ct[0:1, :] + carries[hd]

            return (logits, cumsum, weigh), (0, 3, 6)

        def mla_item(hd):
            def scores():
                cols = slice(hd * MLA_QK_PAD, (hd + 1) * MLA_QK_PAD)
                st = _dot(mk_ref[0, pl.ds(k0, tk), cols], mqt_ref[0, cols, :])
                sts[hd] = jnp.where(causal, st, -jnp.inf) if diag else st

            def weigh():
                st = sts.pop(hd)
                mt = jnp.max(st, axis=0, keepdims=True)
                m_new = mt if diag else jnp.maximum(stats[hd][0], mt)
                p = jnp.exp2((st - m_new) * MLA_SCALE_LOG2E)
                ps = jnp.sum(p, axis=0, keepdims=True)
                contrib = _dot(mvt_ref[0, hd * V_HEAD_DIM:(hd + 1) * V_HEAD_DIM, pl.ds(k0, tk)], p.astype(BF16))
                if diag:
                    macc_ref[hd] = contrib
                    l_new = ps
                else:
                    alpha = jnp.exp2((stats[hd][0] - m_new) * MLA_SCALE_LOG2E)
                    macc_ref[hd] = alpha * macc_ref[hd] + contrib
                    l_new = alpha * stats[hd][1] + ps
                new_s[hd] = (m_new, l_new)

            return (scores, weigh), (0, 3)

        items = []
        for hd in range(SB_HEADS):
            items.append(sb_item(hd))
            items.append(mla_item(hd))
        _staged(items)
        return (tuple(new_c[hd] for hd in range(SB_HEADS)), tuple(new_s[hd] for hd in range(MLA_HEADS)))

    state = tile(pl.multiple_of(qb * tk, tk), None, True)

    def body(i, state):
        return tile(pl.multiple_of((qb - 1 - i) * tk, tk), state, False)

    _, stats = lax.fori_loop(0, qb, body, state)
    for hd in range(MLA_HEADS):
        macc_ref[hd] = macc_ref[hd] / stats[hd][1]
    o_ref[0, :, :SB_WIDTH] = _group_norm(sacc_ref, gs_ref[...])
    o_ref[0, :, SB_WIDTH:] = _group_norm(macc_ref, gm_ref[...])


def _attn_call(sbqt, sbk, sbvt, tri, g_sb, mqt, mk, mvt, g_mla):
    b, s, w = sbk.shape
    qw = mk.shape[2]
    tq = ATT_TQ
    qcol = lambda r: pl.BlockSpec((1, r, tq), lambda i, j: (i, 0, j))
    seq = lambda r, c: pl.BlockSpec((1, r, c), lambda i, j: (i, 0, 0))
    return pl.pallas_call(
        _attn_kernel,
        grid=(b, s // tq),
        in_specs=[qcol(w), seq(s, w), seq(w, s), _const_spec(tri.shape), _const_spec(g_sb.shape),
                  qcol(qw), seq(s, qw), seq(w, s), _const_spec(g_mla.shape)],
        out_specs=pl.BlockSpec((1, tq, 2 * w), lambda i, j: (i, j, 0)),
        out_shape=jax.ShapeDtypeStruct((b, s, 2 * w), BF16),
        scratch_shapes=[pltpu.VMEM((SB_HEADS, LANES, tq), BF16),
                        pltpu.VMEM((SB_HEADS, SB_HEAD_DIM, tq), F32),
                        pltpu.VMEM((MLA_HEADS, V_HEAD_DIM, tq), F32)],
        compiler_params=pltpu.CompilerParams(dimension_semantics=("arbitrary", "arbitrary"),
                                             vmem_limit_bytes=VMEM_LIMIT),
        name="attn",
    )(sbqt, sbk, sbvt, tri, g_sb, mqt, mk, mvt, g_mla)


def _gelu_tanh(x):
    return 0.5 * x * (1.0 + jnp.tanh(0.7978845608028654 * (x + 0.044715 * (x * x * x))))


def _ffn_kernel(x_ref, n_ref, p_ref, w_o_ref, g_mix_ref, g_pre_ref, w_up_ref, cw_ref, cb_ref, w_down_ref,
                g_post_ref, w_ple_ref, g_gate_ref, w_gate_ref, b_gate_ref, g_ple_ref, o_ref,
                halo_ref, facc_ref, *ubuf_refs):
    tm, ch, hl = FFN_TM, FFN_CH, CONV_HALO
    j = pl.program_id(1)

    @pl.when(j == 0)
    def _():
        halo_ref[...] = jnp.zeros_like(halo_ref)

    o_ref[0] = x_ref[0] + _rms(_dot(n_ref[0], w_o_ref[...]), g_mix_ref[...])
    e = _dot(p_ref[0].astype(BF16), w_ple_ref[...])
    h = _rms(o_ref[0], g_pre_ref[...]).astype(BF16)

    def up(c):
        ubuf_ref = ubuf_refs[c % FFN_RING]
        for part in range(2):
            c0 = part * D_FF + c * ch
            pc = slice(part * ch, (part + 1) * ch)
            ubuf_ref[0:hl, pc] = halo_ref[c, :, pc]
            ubuf_ref[hl:hl + tm, pc] = _dot(h, w_up_ref[:, c0:c0 + ch])
        halo_ref[c] = ubuf_ref[tm:tm + hl, :]

    def down(c):
        ubuf_ref = ubuf_refs[c % FFN_RING]
        ys = []
        for part in range(2):
            c0 = part * D_FF + c * ch
            pc = slice(part * ch, (part + 1) * ch)
            y = cb_ref[:, c0:c0 + ch]
            for k in range(CONV_WIDTH):
                sh = hl - (CONV_WIDTH - 1) + k
                y = y + cw_ref[k:k + 1, c0:c0 + ch] * ubuf_ref[sh:sh + tm, pc]
            ys.append(y)
        act = (_gelu_tanh(ys[0]) * ys[1]).astype(BF16)
        contrib = _dot(act, w_down_ref[c * ch:(c + 1) * ch, :])
        if c == 0:
            facc_ref[...] = contrib
        else:
            facc_ref[...] += contrib

    _staged([((lambda c=c: up(c), lambda c=c: down(c)), (0, FFN_RING - 1)) for c in range(D_FF // ch)])

    x2 = o_ref[0] + _rms(facc_ref[...], g_post_ref[...])
    gl = _dot(_rms(x2, g_gate_ref[...]).astype(BF16), w_gate_ref[...]) + b_gate_ref[...]
    o_ref[0] = x2 + _rms(jax.nn.sigmoid(gl) * e, g_ple_ref[...])


def _ffn_call(x, n, p, w_o, g_mix, g_pre, w_up, cw, cb, w_down, g_post, w_ple, g_gate, w_gate, b_gate, g_ple):
    b, s, d = x.shape
    tm = FFN_TM
    tok = lambda w: pl.BlockSpec((1, tm, w), lambda i, j: (i, j, 0))
    consts = [w_o, g_mix, g_pre, w_up, cw, cb, w_down, g_post, w_ple, g_gate, w_gate, b_gate, g_ple]
    return pl.pallas_call(
        _ffn_kernel,
        grid=(b, s // tm),
        in_specs=[tok(d), tok(n.shape[2]), tok(PLE_DIM)] + [
            pl.BlockSpec(a.shape, lambda *_: (0, 0), pipeline_mode=pl.Buffered(1)) for a in consts],
        out_specs=tok(d),
        out_shape=jax.ShapeDtypeStruct((b, s, d), F32),
        scratch_shapes=[pltpu.VMEM((D_FF // FFN_CH, CONV_HALO, 2 * FFN_CH), F32),
                        pltpu.VMEM((tm, d), F32)]
        + [pltpu.VMEM((tm + CONV_HALO, 2 * FFN_CH), F32)] * FFN_RING,
        compiler_params=pltpu.CompilerParams(dimension_semantics=("arbitrary", "arbitrary"),
                                             vmem_limit_bytes=VMEM_LIMIT),
        name="ffn_ple",
    )(x, n, p, *consts)


def _layer(x, p_i, posr, invf_col, tri, w_in, g_pre_mix, g_q_lat, w_q_up, g_kv_lat,
           w_kv_up, g_grp_sb, g_grp_mla, w_o, g_post_mix, g_pre_ffn, w_up, conv_w, conv_b, w_down,
           g_post_ffn, w_ple, g_ple_gate, w_ple_gate, b_ple_gate, g_post_ple):
    b, s, d = x.shape
    row = lambda v: v.reshape(1, -1)
    o_k, o_v = SB_WIDTH, 2 * SB_WIDTH
    o_ql = 3 * SB_WIDTH
    o_kr = o_ql + Q_LORA_RANK + KV_LORA_RANK
    w_kr_pad = jnp.pad(w_in[:, o_kr:], ((0, 0), (QK_NOPE_DIM, MLA_QK_PAD - QK_NOPE_DIM - QK_ROPE_DIM)))
    w_a = jnp.concatenate([w_in[:, o_k:o_v], w_in[:, o_ql:o_kr]], axis=1).astype(BF16)
    w_bt = jnp.concatenate([w_in[:, :o_k], w_in[:, o_v:o_ql], w_kr_pad], axis=1).T.astype(BF16)
    qk = QK_NOPE_DIM + QK_ROPE_DIM
    w_qupt = jnp.pad(w_q_up.reshape(Q_LORA_RANK, MLA_HEADS, qk),
                     ((0, 0), (0, 0), (0, MLA_QK_PAD - qk))).reshape(Q_LORA_RANK, -1).T.astype(BF16)
    w_kv3 = w_kv_up.reshape(KV_LORA_RANK, MLA_HEADS, QK_NOPE_DIM + V_HEAD_DIM)
    w_kn = jnp.pad(w_kv3[:, :, :QK_NOPE_DIM],
                   ((0, 0), (0, 0), (0, MLA_QK_PAD - QK_NOPE_DIM))).reshape(KV_LORA_RANK, -1).astype(BF16)
    w_vt = w_kv3[:, :, QK_NOPE_DIM:].reshape(KV_LORA_RANK, -1).T.astype(BF16)

    sbqt, sbk, sbvt, mqt, mk, mvt = _proj_call(
        x, posr, row(g_pre_mix), w_a, w_bt, row(g_q_lat), w_qupt, row(g_kv_lat), w_kn, w_vt, invf_col)
    n = _attn_call(sbqt, sbk, sbvt, tri, row(g_grp_sb), mqt, mk, mvt, row(g_grp_mla))
    return _ffn_call(x, n, p_i, w_o.astype(BF16), row(g_post_mix), row(g_pre_ffn), w_up.astype(BF16), conv_w, row(conv_b),
                     w_down.astype(BF16), row(g_post_ffn), w_ple.astype(BF16), row(g_ple_gate),
                     w_ple_gate.astype(BF16), row(b_ple_gate), row(g_post_ple))


def kernel(x, p, positions, w_in, g_pre_mix, g_q_lat, w_q_up, g_kv_lat, w_kv_up, g_grp_sb, g_grp_mla, w_o, g_post_mix, g_pre_ffn, w_up, conv_w, conv_b, w_down, g_post_ffn, w_ple, g_ple_gate, w_ple_gate, b_ple_gate, g_post_ple):
    depth = p.shape[0]
    pos_f = positions.astype(F32)
    posr = pos_f[:, None, :]
    inv_freq = 1.0 / (ROPE_THETA ** (jnp.arange(0, QK_ROPE_DIM, 2, dtype=F32) / QK_ROPE_DIM))
    invf_col = inv_freq.reshape(-1, 1)
    kk = jnp.arange(ATT_TK)
    tri = (kk[None, :] >= kk[:, None]).astype(BF16)
    params = (w_in, g_pre_mix, g_q_lat, w_q_up, g_kv_lat, w_kv_up, g_grp_sb, g_grp_mla, w_o, g_post_mix,
              g_pre_ffn, w_up, conv_w, conv_b, w_down, g_post_ffn, w_ple, g_ple_gate, w_ple_gate,
              b_ple_gate, g_post_ple)
    for i in range(depth):
        x = _layer(x, p[i], posr, invf_col, tri, *[w[i] for w in params])
    return x
```

```python
import jax
import jax.numpy as jnp
from jax import lax
from jax.experimental import pallas as pl
from jax.experimental.pallas import tpu as pltpu

F32 = jnp.float32
BF16 = jnp.bfloat16

D_MODEL = 1024
SB_HEADS = 8
SB_HEAD_DIM = 64
MLA_HEADS = 8
QK_NOPE_DIM = 64
QK_ROPE_DIM = 32
V_HEAD_DIM = 64
Q_LORA_RANK = 256
KV_LORA_RANK = 128
SB_WIDTH = SB_HEADS * SB_HEAD_DIM
MLA_WIDTH = MLA_HEADS * V_HEAD_DIM
D_FF = 2816
CONV_WIDTH = 3
PLE_DIM = 256
ROPE_THETA = 10000.0
NORM_EPS = 1e-6

LANES = 128
MLA_QK_PAD = LANES
MLA_QK_WIDTH = MLA_HEADS * MLA_QK_PAD
ROPE_HALF = QK_ROPE_DIM // 2
SB_SCALE = SB_HEAD_DIM ** -0.5
MLA_SCALE_LOG2E = (QK_NOPE_DIM + QK_ROPE_DIM) ** -0.5 * 1.4426950408889634

PROJ_TM = 512
ATT_TQ = 256
ATT_TK = 256
FFN_TM = 512
FFN_CH = 256
CONV_HALO = 8
FFN_RING = 11
VMEM_LIMIT = 56 * 1024 * 1024


def _rms(x, g):
    ms = jnp.mean(x * x, axis=-1, keepdims=True)
    return x * lax.rsqrt(ms + NORM_EPS) * g


def _dot(a, b):
    return jnp.dot(a, b, preferred_element_type=F32)


def _dot_nt(a, b):
    return lax.dot_general(a, b, (((1,), (1,)), ((), ())), preferred_element_type=F32)


def _proj_kernel(x_ref, posr_ref, g_pre_ref, w_a_ref, w_bt_ref, g_q_ref, w_qupt_ref,
                 g_kv_ref, w_kn_ref, w_vt_ref, invf_col_ref,
                 sbqt_ref, sbk_ref, sbvt_ref, mqt_ref, mk_ref, mvt_ref):
    h = _rms(x_ref[0], g_pre_ref[...]).astype(BF16)
    pa = _dot(h, w_a_ref[...])
    sbk_ref[0] = pa[:, :SB_WIDTH].astype(BF16)
    pbt = _dot_nt(w_bt_ref[...], h)
    sbqt_ref[0] = (pbt[:SB_WIDTH] * SB_SCALE).astype(BF16)
    sbvt_ref[0] = pbt[SB_WIDTH:2 * SB_WIDTH].astype(BF16)

    angt = invf_col_ref[...] * posr_ref[0]
    cost = jnp.cos(angt)
    sint = jnp.sin(angt)

    def rope_rows(t, r1):
        x1 = t[r1:r1 + ROPE_HALF]
        x2 = t[r1 + ROPE_HALF:r1 + QK_ROPE_DIM]
        return x1 * cost - x2 * sint, x2 * cost + x1 * sint

    c0 = SB_WIDTH
    nq = _rms(pa[:, c0:c0 + Q_LORA_RANK], g_q_ref[...]).astype(BF16)
    qt = _dot_nt(w_qupt_ref[...], nq)
    for hd in range(MLA_HEADS):
        r0 = hd * MLA_QK_PAD
        r1 = r0 + QK_NOPE_DIM
        y1, y2 = rope_rows(qt, r1)
        mqt_ref[0, r0:r1, :] = qt[r0:r1].astype(BF16)
        mqt_ref[0, r1:r1 + ROPE_HALF, :] = y1.astype(BF16)
        mqt_ref[0, r1 + ROPE_HALF:r1 + QK_ROPE_DIM, :] = y2.astype(BF16)
        mqt_ref[0, r1 + QK_ROPE_DIM:r0 + MLA_QK_PAD, :] = qt[r1 + QK_ROPE_DIM:r0 + MLA_QK_PAD].astype(BF16)

    c1 = c0 + Q_LORA_RANK
    nkv = _rms(pa[:, c1:c1 + KV_LORA_RANK], g_kv_ref[...]).astype(BF16)
    mvt_ref[0] = _dot_nt(w_vt_ref[...], nkv).astype(BF16)
    kn = _dot(nkv, w_kn_ref[...])
    krt = pbt[2 * SB_WIDTH:]
    y1, y2 = rope_rows(krt, QK_NOPE_DIM)
    krr = jnp.concatenate([krt[:QK_NOPE_DIM], y1, y2, krt[QK_NOPE_DIM + QK_ROPE_DIM:]], axis=0).T
    for hd in range(MLA_HEADS):
        cols = slice(hd * MLA_QK_PAD, (hd + 1) * MLA_QK_PAD)
        mk_ref[0, :, cols] = (kn[:, cols] + krr).astype(BF16)


def _const_spec(shape):
    return pl.BlockSpec(shape, lambda *_: (0,) * len(shape))


def _proj_call(x, posr, g_pre, w_a, w_bt, g_q, w_qupt, g_kv, w_kn, w_vt, invf_col):
    b, s, d = x.shape
    tm = PROJ_TM
    tok = lambda w: pl.BlockSpec((1, tm, w), lambda i, j: (i, j, 0))
    tok_t = lambda r: pl.BlockSpec((1, r, tm), lambda i, j: (i, 0, j))
    return pl.pallas_call(
        _proj_kernel,
        grid=(b, s // tm),
        in_specs=[tok(d), tok_t(1), _const_spec(g_pre.shape), _const_spec(w_a.shape),
                  _const_spec(w_bt.shape), _const_spec(g_q.shape), _const_spec(w_qupt.shape),
                  _const_spec(g_kv.shape), _const_spec(w_kn.shape), _const_spec(w_vt.shape),
                  _const_spec(invf_col.shape)],
        out_specs=[tok_t(SB_WIDTH), tok(SB_WIDTH), tok_t(SB_WIDTH),
                   tok_t(MLA_QK_WIDTH), tok(MLA_QK_WIDTH), tok_t(MLA_WIDTH)],
        out_shape=[jax.ShapeDtypeStruct((b, SB_WIDTH, s), BF16),
                   jax.ShapeDtypeStruct((b, s, SB_WIDTH), BF16),
                   jax.ShapeDtypeStruct((b, SB_WIDTH, s), BF16),
                   jax.ShapeDtypeStruct((b, MLA_QK_WIDTH, s), BF16),
                   jax.ShapeDtypeStruct((b, s, MLA_QK_WIDTH), BF16),
                   jax.ShapeDtypeStruct((b, MLA_WIDTH, s), BF16)],
        compiler_params=pltpu.CompilerParams(dimension_semantics=("arbitrary", "arbitrary"),
                                             vmem_limit_bytes=VMEM_LIMIT),
        name="proj",
    )(x, posr, g_pre, w_a, w_bt, g_q, w_qupt, g_kv, w_kn, w_vt, invf_col)


def _staged(items):
    nsteps = max(i + delays[-1] for i, (_, delays) in enumerate(items)) + 1
    for step in range(nsteps):
        for i, (fns, delays) in enumerate(items):
            for s in reversed(range(len(fns))):
                if step - delays[s] == i:
                    fns[s]()


def _group_norm(acc_ref, g):
    ot = jnp.concatenate([acc_ref[hd] for hd in range(acc_ref.shape[0])], axis=0)
    return _rms(ot.T, g).astype(BF16)


def _softplus(z):
    neg_abs = pltpu.bitcast(pltpu.bitcast(z, jnp.uint32) | jnp.uint32(0x80000000), F32)
    return jnp.maximum(z, 0.0) + jnp.log(1.0 + jnp.exp(neg_abs))


def _attn_kernel(sqt_ref, sk_ref, svt_ref, tri_ref, gs_ref, mqt_ref, mk_ref, mvt_ref, gm_ref, o_ref,
                 qw_ref, sacc_ref, macc_ref, carry_ref, max_ref, sum_ref):
    tq, tk = ATT_TQ, ATT_TK
    qb = pl.program_id(1)
    row = lax.broadcasted_iota(jnp.int32, (LANES, tq), 0)
    for hd in range(SB_HEADS):
        pair = hd // 2
        blk = sqt_ref[0, pair * LANES:(pair + 1) * LANES, :]
        keep = (row < SB_HEAD_DIM) if hd % 2 == 0 else (row >= SB_HEAD_DIM)
        qw_ref[hd] = jnp.where(keep, blk, jnp.zeros_like(blk))
    tri = tri_ref[...]
    key_i = lax.broadcasted_iota(jnp.int32, (tk, tq), 0)
    qry_i = lax.broadcasted_iota(jnp.int32, (tk, tq), 1)
    strict = key_i < qry_i
    causal = key_i <= qry_i

    def tile_items(k0, diag):
        zts, cts, sts = {}, {}, {}

        def sb_item(hd):
            def logits():
                pair = hd // 2
                kp = sk_ref[0, pl.ds(k0, tk), pair * LANES:(pair + 1) * LANES]
                zts[hd] = _dot(kp, qw_ref[hd])

            def cumsum():
                sp = _softplus(zts[hd])
                if diag:
                    sp = jnp.where(strict, sp, 0.0)
                cts[hd] = _dot(tri, sp.astype(BF16))

            def weigh():
                ct = cts.pop(hd)
                wt = jnp.exp(zts.pop(hd) - ct)
                if diag:
                    wt = jnp.where(strict, wt, 0.0)
                contrib = _dot(svt_ref[0, hd * SB_HEAD_DIM:(hd + 1) * SB_HEAD_DIM, pl.ds(k0, tk)],
                               wt.astype(BF16))
                if diag:
                    sacc_ref[hd] = contrib
                    carry_ref[hd] = ct[0:1, :]
                else:
                    carry = carry_ref[hd]
                    sacc_ref[hd] += contrib * jnp.exp(-carry)
                    carry_ref[hd] = ct[0:1, :] + carry

            return (logits, cumsum, weigh), (0, 3, 6)

        def mla_item(hd):
            def scores():
                cols = slice(hd * MLA_QK_PAD, (hd + 1) * MLA_QK_PAD)
                st = _dot(mk_ref[0, pl.ds(k0, tk), cols], mqt_ref[0, cols, :])
                sts[hd] = jnp.where(causal, st, -jnp.inf) if diag else st

            def weigh():
                st = sts.pop(hd)
                mt = jnp.max(st, axis=0, keepdims=True)
                m_old = None if diag else max_ref[hd]
                m_new = mt if diag else jnp.maximum(m_old, mt)
                p = jnp.exp2((st - m_new) * MLA_SCALE_LOG2E)
                ps = jnp.sum(p, axis=0, keepdims=True)
                contrib = _dot(mvt_ref[0, hd * V_HEAD_DIM:(hd + 1) * V_HEAD_DIM, pl.ds(k0, tk)], p.astype(BF16))
                if diag:
                    macc_ref[hd] = contrib
                    sum_ref[hd] = ps
                else:
                    alpha = jnp.exp2((m_old - m_new) * MLA_SCALE_LOG2E)
                    macc_ref[hd] = alpha * macc_ref[hd] + contrib
                    sum_ref[hd] = alpha * sum_ref[hd] + ps
                max_ref[hd] = m_new

            return (scores, weigh), (0, 3)

        items = []
        for hd in range(SB_HEADS):
            items.append(sb_item(hd))
            items.append(mla_item(hd))
        return items

    _staged(tile_items(pl.multiple_of(qb * tk, tk), True))

    def pair_body(i, _):
        k_hi = pl.multiple_of((qb - 1 - 2 * i) * tk, tk)
        k_lo = pl.multiple_of((qb - 2 - 2 * i) * tk, tk)
        _staged(tile_items(k_hi, False) + tile_items(k_lo, False))
        return 0

    lax.fori_loop(0, qb // 2, pair_body, 0)

    @pl.when(qb % 2 == 1)
    def _():
        _staged(tile_items(0, False))

    for hd in range(MLA_HEADS):
        macc_ref[hd] = macc_ref[hd] / sum_ref[hd]
    o_ref[0, :, :SB_WIDTH] = _group_norm(sacc_ref, gs_ref[...])
    o_ref[0, :, SB_WIDTH:] = _group_norm(macc_ref, gm_ref[...])


def _attn_call(sbqt, sbk, sbvt, tri, g_sb, mqt, mk, mvt, g_mla):
    b, s, w = sbk.shape
    qw = mk.shape[2]
    tq = ATT_TQ
    qcol = lambda r: pl.BlockSpec((1, r, tq), lambda i, j: (i, 0, j))
    seq = lambda r, c: pl.BlockSpec((1, r, c), lambda i, j: (i, 0, 0))
    return pl.pallas_call(
        _attn_kernel,
        grid=(b, s // tq),
        in_specs=[qcol(w), seq(s, w), seq(w, s), _const_spec(tri.shape), _const_spec(g_sb.shape),
                  qcol(qw), seq(s, qw), seq(w, s), _const_spec(g_mla.shape)],
        out_specs=pl.BlockSpec((1, tq, 2 * w), lambda i, j: (i, j, 0)),
        out_shape=jax.ShapeDtypeStruct((b, s, 2 * w), BF16),
        scratch_shapes=[pltpu.VMEM((SB_HEADS, LANES, tq), BF16),
                        pltpu.VMEM((SB_HEADS, SB_HEAD_DIM, tq), F32),
                        pltpu.VMEM((MLA_HEADS, V_HEAD_DIM, tq), F32),
                        pltpu.VMEM((SB_HEADS, 1, tq), F32),
                        pltpu.VMEM((MLA_HEADS, 1, tq), F32),
                        pltpu.VMEM((MLA_HEADS, 1, tq), F32)],
        compiler_params=pltpu.CompilerParams(dimension_semantics=("arbitrary", "arbitrary"),
                                             vmem_limit_bytes=VMEM_LIMIT),
        name="attn",
    )(sbqt, sbk, sbvt, tri, g_sb, mqt, mk, mvt, g_mla)


def _gelu_tanh(x):
    return 0.5 * x * (1.0 + jnp.tanh(0.7978845608028654 * (x + 0.044715 * (x * x * x))))


def _ffn_kernel(x_ref, n_ref, p_ref, w_o_ref, g_mix_ref, g_pre_ref, w_up_ref, cw_ref, cb_ref, w_down_ref,
                g_post_ref, w_ple_ref, g_gate_ref, w_gate_ref, b_gate_ref, g_ple_ref, o_ref,
                halo_ref, facc_ref, *ubuf_refs):
    tm, ch, hl = FFN_TM, FFN_CH, CONV_HALO
    j = pl.program_id(1)

    @pl.when(j == 0)
    def _():
        halo_ref[...] = jnp.zeros_like(halo_ref)

    o_ref[0] = x_ref[0] + _rms(_dot(n_ref[0], w_o_ref[...]), g_mix_ref[...])
    e = _dot(p_ref[0].astype(BF16), w_ple_ref[...])
    h = _rms(o_ref[0], g_pre_ref[...]).astype(BF16)

    def up(c):
        ubuf_ref = ubuf_refs[c % FFN_RING]
        for part in range(2):
            c0 = part * D_FF + c * ch
            pc = slice(part * ch, (part + 1) * ch)
            ubuf_ref[0:hl, pc] = halo_ref[c, :, pc]
            ubuf_ref[hl:hl + tm, pc] = _dot(h, w_up_ref[:, c0:c0 + ch])
        halo_ref[c] = ubuf_ref[tm:tm + hl, :]

    def down(c):
        ubuf_ref = ubuf_refs[c % FFN_RING]
        ys = []
        for part in range(2):
            c0 = part * D_FF + c * ch
            pc = slice(part * ch, (part + 1) * ch)
            y = cb_ref[:, c0:c0 + ch]
            for k in range(CONV_WIDTH):
                sh = hl - (CONV_WIDTH - 1) + k
                y = y + cw_ref[k:k + 1, c0:c0 + ch] * ubuf_ref[sh:sh + tm, pc]
            ys.append(y)
        act = (_gelu_tanh(ys[0]) * ys[1]).astype(BF16)
        contrib = _dot(act, w_down_ref[c * ch:(c + 1) * ch, :])
        if c == 0:
            facc_ref[...] = contrib
        else:
            facc_ref[...] += contrib

    _staged([((lambda c=c: up(c), lambda c=c: down(c)), (0, FFN_RING - 1)) for c in range(D_FF // ch)])

    x2 = o_ref[0] + _rms(facc_ref[...], g_post_ref[...])
    gl = _dot(_rms(x2, g_gate_ref[...]).astype(BF16), w_gate_ref[...]) + b_gate_ref[...]
    o_ref[0] = x2 + _rms(jax.nn.sigmoid(gl) * e, g_ple_ref[...])


def _ffn_call(x, n, p, w_o, g_mix, g_pre, w_up, cw, cb, w_down, g_post, w_ple, g_gate, w_gate, b_gate, g_ple):
    b, s, d = x.shape
    tm = FFN_TM
    tok = lambda w: pl.BlockSpec((1, tm, w), lambda i, j: (i, j, 0))
    consts = [w_o, g_mix, g_pre, w_up, cw, cb, w_down, g_post, w_ple, g_gate, w_gate, b_gate, g_ple]
    return pl.pallas_call(
        _ffn_kernel,
        grid=(b, s // tm),
        in_specs=[tok(d), tok(n.shape[2]), tok(PLE_DIM)] + [
            pl.BlockSpec(a.shape, lambda *_: (0, 0), pipeline_mode=pl.Buffered(1)) for a in consts],
        out_specs=tok(d),
        out_shape=jax.ShapeDtypeStruct((b, s, d), F32),
        scratch_shapes=[pltpu.VMEM((D_FF // FFN_CH, CONV_HALO, 2 * FFN_CH), F32),
                        pltpu.VMEM((tm, d), F32)]
        + [pltpu.VMEM((tm + CONV_HALO, 2 * FFN_CH), F32)] * FFN_RING,
        compiler_params=pltpu.CompilerParams(dimension_semantics=("arbitrary", "arbitrary"),
                                             vmem_limit_bytes=VMEM_LIMIT),
        name="ffn_ple",
    )(x, n, p, *consts)


def _layer(x, p_i, posr, invf_col, tri, w_in, g_pre_mix, g_q_lat, w_q_up, g_kv_lat,
           w_kv_up, g_grp_sb, g_grp_mla, w_o, g_post_mix, g_pre_ffn, w_up, conv_w, conv_b, w_down,
           g_post_ffn, w_ple, g_ple_gate, w_ple_gate, b_ple_gate, g_post_ple):
    b, s, d = x.shape
    row = lambda v: v.reshape(1, -1)
    o_k, o_v = SB_WIDTH, 2 * SB_WIDTH
    o_ql = 3 * SB_WIDTH
    o_kr = o_ql + Q_LORA_RANK + KV_LORA_RANK
    w_kr_pad = jnp.pad(w_in[:, o_kr:], ((0, 0), (QK_NOPE_DIM, MLA_QK_PAD - QK_NOPE_DIM - QK_ROPE_DIM)))
    w_a = jnp.concatenate([w_in[:, o_k:o_v], w_in[:, o_ql:o_kr]], axis=1).astype(BF16)
    w_bt = jnp.concatenate([w_in[:, :o_k], w_in[:, o_v:o_ql], w_kr_pad], axis=1).T.astype(BF16)
    qk = QK_NOPE_DIM + QK_ROPE_DIM
    w_qupt = jnp.pad(w_q_up.reshape(Q_LORA_RANK, MLA_HEADS, qk),
                     ((0, 0), (0, 0), (0, MLA_QK_PAD - qk))).reshape(Q_LORA_RANK, -1).T.astype(BF16)
    w_kv3 = w_kv_up.reshape(KV_LORA_RANK, MLA_HEADS, QK_NOPE_DIM + V_HEAD_DIM)
    w_kn = jnp.pad(w_kv3[:, :, :QK_NOPE_DIM],
                   ((0, 0), (0, 0), (0, MLA_QK_PAD - QK_NOPE_DIM))).reshape(KV_LORA_RANK, -1).astype(BF16)
    w_vt = w_kv3[:, :, QK_NOPE_DIM:].reshape(KV_LORA_RANK, -1).T.astype(BF16)

    sbqt, sbk, sbvt, mqt, mk, mvt = _proj_call(
        x, posr, row(g_pre_mix), w_a, w_bt, row(g_q_lat), w_qupt, row(g_kv_lat), w_kn, w_vt, invf_col)
    n = _attn_call(sbqt, sbk, sbvt, tri, row(g_grp_sb), mqt, mk, mvt, row(g_grp_mla))
    return _ffn_call(x, n, p_i, w_o.astype(BF16), row(g_post_mix), row(g_pre_ffn), w_up.astype(BF16), conv_w, row(conv_b),
                     w_down.astype(BF16), row(g_post_ffn), w_ple.astype(BF16), row(g_ple_gate),
                     w_ple_gate.astype(BF16), row(b_ple_gate), row(g_post_ple))


def kernel(x, p, positions, w_in, g_pre_mix, g_q_lat, w_q_up, g_kv_lat, w_kv_up, g_grp_sb, g_grp_mla, w_o, g_post_mix, g_pre_ffn, w_up, conv_w, conv_b, w_down, g_post_ffn, w_ple, g_ple_gate, w_ple_gate, b_ple_gate, g_post_ple):
    depth = p.shape[0]
    pos_f = positions.astype(F32)
    posr = pos_f[:, None, :]
    inv_freq = 1.0 / (ROPE_THETA ** (jnp.arange(0, QK_ROPE_DIM, 2, dtype=F32) / QK_ROPE_DIM))
    invf_col = inv_freq.reshape(-1, 1)
    kk = jnp.arange(ATT_TK)
    tri = (kk[None, :] >= kk[:, None]).astype(BF16)
    params = (w_in, g_pre_mix, g_q_lat, w_q_up, g_kv_lat, w_kv_up, g_grp_sb, g_grp_mla, w_o, g_post_mix,
              g_pre_ffn, w_up, conv_w, conv_b, w_down, g_post_ffn, w_ple, g_ple_gate, w_ple_gate,
              b_ple_gate, g_post_ple)
    for i in range(depth):
        x = _layer(x, p[i], posr, invf_col, tri, *[w[i] for w in params])
    return x
```

```python
import jax
import jax.numpy as jnp
from jax import lax
from jax.experimental import pallas as pl
from jax.experimental.pallas import tpu as pltpu

F32 = jnp.float32
BF16 = jnp.bfloat16

D_MODEL = 1024
SB_HEADS = 8
SB_HEAD_DIM = 64
MLA_HEADS = 8
QK_NOPE_DIM = 64
QK_ROPE_DIM = 32
V_HEAD_DIM = 64
Q_LORA_RANK = 256
KV_LORA_RANK = 128
SB_WIDTH = SB_HEADS * SB_HEAD_DIM
MLA_WIDTH = MLA_HEADS * V_HEAD_DIM
D_FF = 2816
CONV_WIDTH = 3
PLE_DIM = 256
ROPE_THETA = 10000.0
NORM_EPS = 1e-6

LANES = 128
MLA_QK_PAD = LANES
MLA_QK_WIDTH = MLA_HEADS * MLA_QK_PAD
ROPE_HALF = QK_ROPE_DIM // 2
SB_SCALE = SB_HEAD_DIM ** -0.5
MLA_SCALE_LOG2E = (QK_NOPE_DIM + QK_ROPE_DIM) ** -0.5 * 1.4426950408889634

PROJ_TM = 512
ATT_TQ = 256
ATT_TK = 256
FFN_TM = 512
FFN_CH = 256
CONV_HALO = 8
FFN_RING = 11
VMEM_LIMIT = 56 * 1024 * 1024


def _rms(x, g):
    ms = jnp.mean(x * x, axis=-1, keepdims=True)
    return x * lax.rsqrt(ms + NORM_EPS) * g


def _dot(a, b):
    return jnp.dot(a, b, preferred_element_type=F32)


def _dot_nt(a, b):
    return lax.dot_general(a, b, (((1,), (1,)), ((), ())), preferred_element_type=F32)


def _proj_kernel(x_ref, posr_ref, g_pre_ref, w_a_ref, w_bt_ref, g_q_ref, w_qupt_ref,
                 g_kv_ref, w_kn_ref, w_vt_ref, invf_col_ref,
                 sbqt_ref, sbk_ref, sbvt_ref, mqt_ref, mk_ref, mvt_ref):
    h = _rms(x_ref[0], g_pre_ref[...]).astype(BF16)
    pa = _dot(h, w_a_ref[...])
    sbk_ref[0] = pa[:, :SB_WIDTH].astype(BF16)
    pbt = _dot_nt(w_bt_ref[...], h)
    sbqt_ref[0] = (pbt[:SB_WIDTH] * SB_SCALE).astype(BF16)
    sbvt_ref[0] = pbt[SB_WIDTH:2 * SB_WIDTH].astype(BF16)

    angt = invf_col_ref[...] * posr_ref[0]
    cost = jnp.cos(angt)
    sint = jnp.sin(angt)

    def rope_rows(t, r1):
        x1 = t[r1:r1 + ROPE_HALF]
        x2 = t[r1 + ROPE_HALF:r1 + QK_ROPE_DIM]
        return x1 * cost - x2 * sint, x2 * cost + x1 * sint

    c0 = SB_WIDTH
    nq = _rms(pa[:, c0:c0 + Q_LORA_RANK], g_q_ref[...]).astype(BF16)
    qt = _dot_nt(w_qupt_ref[...], nq)
    for hd in range(MLA_HEADS):
        r0 = hd * MLA_QK_PAD
        r1 = r0 + QK_NOPE_DIM
        y1, y2 = rope_rows(qt, r1)
        mqt_ref[0, r0:r1, :] = (qt[r0:r1] * MLA_SCALE_LOG2E).astype(BF16)
        mqt_ref[0, r1:r1 + ROPE_HALF, :] = (y1 * MLA_SCALE_LOG2E).astype(BF16)
        mqt_ref[0, r1 + ROPE_HALF:r1 + QK_ROPE_DIM, :] = (y2 * MLA_SCALE_LOG2E).astype(BF16)
        mqt_ref[0, r1 + QK_ROPE_DIM:r0 + MLA_QK_PAD, :] = qt[r1 + QK_ROPE_DIM:r0 + MLA_QK_PAD].astype(BF16)

    c1 = c0 + Q_LORA_RANK
    nkv = _rms(pa[:, c1:c1 + KV_LORA_RANK], g_kv_ref[...]).astype(BF16)
    mvt_ref[0] = _dot_nt(w_vt_ref[...], nkv).astype(BF16)
    kn = _dot(nkv, w_kn_ref[...])
    krt = pbt[2 * SB_WIDTH:]
    y1, y2 = rope_rows(krt, QK_NOPE_DIM)
    krr = jnp.concatenate([krt[:QK_NOPE_DIM], y1, y2, krt[QK_NOPE_DIM + QK_ROPE_DIM:]], axis=0).T
    for hd in range(MLA_HEADS):
        cols = slice(hd * MLA_QK_PAD, (hd + 1) * MLA_QK_PAD)
        mk_ref[0, :, cols] = (kn[:, cols] + krr).astype(BF16)


def _const_spec(shape):
    return pl.BlockSpec(shape, lambda *_: (0,) * len(shape))


def _proj_call(x, posr, g_pre, w_a, w_bt, g_q, w_qupt, g_kv, w_kn, w_vt, invf_col):
    b, s, d = x.shape
    tm = PROJ_TM
    tok = lambda w: pl.BlockSpec((1, tm, w), lambda i, j: (i, j, 0))
    tok_t = lambda r: pl.BlockSpec((1, r, tm), lambda i, j: (i, 0, j))
    return pl.pallas_call(
        _proj_kernel,
        grid=(b, s // tm),
        in_specs=[tok(d), tok_t(1), _const_spec(g_pre.shape), _const_spec(w_a.shape),
                  _const_spec(w_bt.shape), _const_spec(g_q.shape), _const_spec(w_qupt.shape),
                  _const_spec(g_kv.shape), _const_spec(w_kn.shape), _const_spec(w_vt.shape),
                  _const_spec(invf_col.shape)],
        out_specs=[tok_t(SB_WIDTH), tok(SB_WIDTH), tok_t(SB_WIDTH),
                   tok_t(MLA_QK_WIDTH), tok(MLA_QK_WIDTH), tok_t(MLA_WIDTH)],
        out_shape=[jax.ShapeDtypeStruct((b, SB_WIDTH, s), BF16),
                   jax.ShapeDtypeStruct((b, s, SB_WIDTH), BF16),
                   jax.ShapeDtypeStruct((b, SB_WIDTH, s), BF16),
                   jax.ShapeDtypeStruct((b, MLA_QK_WIDTH, s), BF16),
                   jax.ShapeDtypeStruct((b, s, MLA_QK_WIDTH), BF16),
                   jax.ShapeDtypeStruct((b, MLA_WIDTH, s), BF16)],
        compiler_params=pltpu.CompilerParams(dimension_semantics=("arbitrary", "arbitrary"),
                                             vmem_limit_bytes=VMEM_LIMIT),
        name="proj",
    )(x, posr, g_pre, w_a, w_bt, g_q, w_qupt, g_kv, w_kn, w_vt, invf_col)


def _staged(items):
    nsteps = max(i + delays[-1] for i, (_, delays) in enumerate(items)) + 1
    for step in range(nsteps):
        for i, (fns, delays) in enumerate(items):
            for s in reversed(range(len(fns))):
                if step - delays[s] == i:
                    fns[s]()


def _group_norm(acc_ref, g):
    ot = jnp.concatenate([acc_ref[hd] for hd in range(acc_ref.shape[0])], axis=0)
    return _rms(ot.T, g).astype(BF16)


def _softplus(z):
    neg_abs = pltpu.bitcast(pltpu.bitcast(z, jnp.uint32) | jnp.uint32(0x80000000), F32)
    return jnp.maximum(z, 0.0) + jnp.log(1.0 + jnp.exp(neg_abs))


def _attn_kernel(sqt_ref, sk_ref, svt_ref, tri_ref, gs_ref, mqt_ref, mk_ref, mvt_ref, gm_ref, o_ref,
                 qw_ref, sacc_ref, macc_ref, carry_ref, max_ref, sum_ref):
    tq, tk = ATT_TQ, ATT_TK
    qb = pl.program_id(1)
    row = lax.broadcasted_iota(jnp.int32, (LANES, tq), 0)
    for hd in range(SB_HEADS):
        pair = hd // 2
        blk = sqt_ref[0, pair * LANES:(pair + 1) * LANES, :]
        keep = (row < SB_HEAD_DIM) if hd % 2 == 0 else (row >= SB_HEAD_DIM)
        qw_ref[hd] = jnp.where(keep, blk, jnp.zeros_like(blk))
    tri = tri_ref[...]
    key_i = lax.broadcasted_iota(jnp.int32, (tk, tq), 0)
    qry_i = lax.broadcasted_iota(jnp.int32, (tk, tq), 1)
    strict = key_i < qry_i
    causal = key_i <= qry_i

    def tile_items(k0, diag):
        zts, cts, sts = {}, {}, {}

        def sb_item(hd):
            def logits():
                pair = hd // 2
                kp = sk_ref[0, pl.ds(k0, tk), pair * LANES:(pair + 1) * LANES]
                zts[hd] = _dot(kp, qw_ref[hd])

            def cumsum():
                sp = _softplus(zts[hd])
                if diag:
                    sp = jnp.where(strict, sp, 0.0)
                cts[hd] = _dot(tri, sp.astype(BF16))

            def weigh():
                ct = cts.pop(hd)
                wt = jnp.exp(zts.pop(hd) - ct)
                if diag:
                    wt = jnp.where(strict, wt, 0.0)
                contrib = _dot(svt_ref[0, hd * SB_HEAD_DIM:(hd + 1) * SB_HEAD_DIM, pl.ds(k0, tk)],
                               wt.astype(BF16))
                if diag:
                    sacc_ref[hd] = contrib
                    carry_ref[hd] = ct[0:1, :]
                else:
                    carry = carry_ref[hd]
                    sacc_ref[hd] += contrib * jnp.exp(-carry)
                    carry_ref[hd] = ct[0:1, :] + carry

            return (logits, cumsum, weigh), (0, 3, 6)

        def mla_item(hd):
            def scores():
                cols = slice(hd * MLA_QK_PAD, (hd + 1) * MLA_QK_PAD)
                st = _dot(mk_ref[0, pl.ds(k0, tk), cols], mqt_ref[0, cols, :])
                sts[hd] = jnp.where(causal, st, -jnp.inf) if diag else st

            def weigh():
                st = sts.pop(hd)
                mt = jnp.max(st, axis=0, keepdims=True)
                m_old = None if diag else max_ref[hd]
                m_new = mt if diag else jnp.maximum(m_old, mt)
                p = jnp.exp2(st - m_new)
                ps = jnp.sum(p, axis=0, keepdims=True)
                contrib = _dot(mvt_ref[0, hd * V_HEAD_DIM:(hd + 1) * V_HEAD_DIM, pl.ds(k0, tk)], p.astype(BF16))
                if diag:
                    macc_ref[hd] = contrib
                    sum_ref[hd] = ps
                else:
                    alpha = jnp.exp2(m_old - m_new)
                    macc_ref[hd] = alpha * macc_ref[hd] + contrib
                    sum_ref[hd] = alpha * sum_ref[hd] + ps
                max_ref[hd] = m_new

            return (scores, weigh), (0, 3)

        items = []
        for hd in range(SB_HEADS):
            items.append(sb_item(hd))
            items.append(mla_item(hd))
        return items

    _staged(tile_items(pl.multiple_of(qb * tk, tk), True))

    def pair_body(i, _):
        k_hi = pl.multiple_of((qb - 1 - 2 * i) * tk, tk)
        k_lo = pl.multiple_of((qb - 2 - 2 * i) * tk, tk)
        _staged(tile_items(k_hi, False) + tile_items(k_lo, False))
        return 0

    lax.fori_loop(0, qb // 2, pair_body, 0)

    @pl.when(qb % 2 == 1)
    def _():
        _staged(tile_items(0, False))

    for hd in range(MLA_HEADS):
        macc_ref[hd] = macc_ref[hd] / sum_ref[hd]
    o_ref[0, :, :SB_WIDTH] = _group_norm(sacc_ref, gs_ref[...])
    o_ref[0, :, SB_WIDTH:] = _group_norm(macc_ref, gm_ref[...])


def _attn_call(sbqt, sbk, sbvt, tri, g_sb, mqt, mk, mvt, g_mla):
    b, s, w = sbk.shape
    qw = mk.shape[2]
    tq = ATT_TQ
    qcol = lambda r: pl.BlockSpec((1, r, tq), lambda i, j: (i, 0, j))
    seq = lambda r, c: pl.BlockSpec((1, r, c), lambda i, j: (i, 0, 0))
    return pl.pallas_call(
        _attn_kernel,
        grid=(b, s // tq),
        in_specs=[qcol(w), seq(s, w), seq(w, s), _const_spec(tri.shape), _const_spec(g_sb.shape),
                  qcol(qw), seq(s, qw), seq(w, s), _const_spec(g_mla.shape)],
        out_specs=pl.BlockSpec((1, tq, 2 * w), lambda i, j: (i, j, 0)),
        out_shape=jax.ShapeDtypeStruct((b, s, 2 * w), BF16),
        scratch_shapes=[pltpu.VMEM((SB_HEADS, LANES, tq), BF16),
                        pltpu.VMEM((SB_HEADS, SB_HEAD_DIM, tq), F32),
                        pltpu.VMEM((MLA_HEADS, V_HEAD_DIM, tq), F32),
                        pltpu.VMEM((SB_HEADS, 1, tq), F32),
                        pltpu.VMEM((MLA_HEADS, 1, tq), F32),
                        pltpu.VMEM((MLA_HEADS, 1, tq), F32)],
        compiler_params=pltpu.CompilerParams(dimension_semantics=("arbitrary", "arbitrary"),
                                             vmem_limit_bytes=VMEM_LIMIT),
        name="attn",
    )(sbqt, sbk, sbvt, tri, g_sb, mqt, mk, mvt, g_mla)


def _gelu_tanh(x):
    return 0.5 * x * (1.0 + jnp.tanh(0.7978845608028654 * (x + 0.044715 * (x * x * x))))


def _ffn_kernel(x_ref, n_ref, p_ref, w_o_ref, g_mix_ref, g_pre_ref, w_up_ref, cw_ref, cb_ref, w_down_ref,
                g_post_ref, w_ple_ref, g_gate_ref, w_gate_ref, b_gate_ref, g_ple_ref, o_ref,
                halo_ref, facc_ref, *ubuf_refs):
    tm, ch, hl = FFN_TM, FFN_CH, CONV_HALO
    j = pl.program_id(1)

    @pl.when(j == 0)
    def _():
        halo_ref[...] = jnp.zeros_like(halo_ref)

    o_ref[0] = x_ref[0] + _rms(_dot(n_ref[0], w_o_ref[...]), g_mix_ref[...])
    e = _dot(p_ref[0].astype(BF16), w_ple_ref[...])
    h = _rms(o_ref[0], g_pre_ref[...]).astype(BF16)

    def up(c):
        ubuf_ref = ubuf_refs[c % FFN_RING]
        for part in range(2):
            c0 = part * D_FF + c * ch
            pc = slice(part * ch, (part + 1) * ch)
            ubuf_ref[0:hl, pc] = halo_ref[c, :, pc]
            ubuf_ref[hl:hl + tm, pc] = _dot(h, w_up_ref[:, c0:c0 + ch])
        halo_ref[c] = ubuf_ref[tm:tm + hl, :]

    def down(c):
        ubuf_ref = ubuf_refs[c % FFN_RING]
        ys = []
        for part in range(2):
            c0 = part * D_FF + c * ch
            pc = slice(part * ch, (part + 1) * ch)
            y = cb_ref[:, c0:c0 + ch]
            for k in range(CONV_WIDTH):
                sh = hl - (CONV_WIDTH - 1) + k
                y = y + cw_ref[k:k + 1, c0:c0 + ch] * ubuf_ref[sh:sh + tm, pc]
            ys.append(y)
        act = (_gelu_tanh(ys[0]) * ys[1]).astype(BF16)
        contrib = _dot(act, w_down_ref[c * ch:(c + 1) * ch, :])
        if c == 0:
            facc_ref[...] = contrib
        else:
            facc_ref[...] += contrib

    _staged([((lambda c=c: up(c), lambda c=c: down(c)), (0, FFN_RING - 1)) for c in range(D_FF // ch)])

    x2 = o_ref[0] + _rms(facc_ref[...], g_post_ref[...])
    gl = _dot(_rms(x2, g_gate_ref[...]).astype(BF16), w_gate_ref[...]) + b_gate_ref[...]
    o_ref[0] = x2 + _rms(jax.nn.sigmoid(gl) * e, g_ple_ref[...])


def _ffn_call(x, n, p, w_o, g_mix, g_pre, w_up, cw, cb, w_down, g_post, w_ple, g_gate, w_gate, b_gate, g_ple):
    b, s, d = x.shape
    tm = FFN_TM
    tok = lambda w: pl.BlockSpec((1, tm, w), lambda i, j: (i, j, 0))
    consts = [w_o, g_mix, g_pre, w_up, cw, cb, w_down, g_post, w_ple, g_gate, w_gate, b_gate, g_ple]
    return pl.pallas_call(
        _ffn_kernel,
        grid=(b, s // tm),
        in_specs=[tok(d), tok(n.shape[2]), tok(PLE_DIM)] + [
            pl.BlockSpec(a.shape, lambda *_: (0, 0), pipeline_mode=pl.Buffered(1)) for a in consts],
        out_specs=tok(d),
        out_shape=jax.ShapeDtypeStruct((b, s, d), F32),
        scratch_shapes=[pltpu.VMEM((D_FF // FFN_CH, CONV_HALO, 2 * FFN_CH), F32),
                        pltpu.VMEM((tm, d), F32)]
        + [pltpu.VMEM((tm + CONV_HALO, 2 * FFN_CH), F32)] * FFN_RING,
        compiler_params=pltpu.CompilerParams(dimension_semantics=("arbitrary", "arbitrary"),
                                             vmem_limit_bytes=VMEM_LIMIT),
        name="ffn_ple",
    )(x, n, p, *consts)


def _layer(x, p_i, posr, invf_col, tri, w_in, g_pre_mix, g_q_lat, w_q_up, g_kv_lat,
           w_kv_up, g_grp_sb, g_grp_mla, w_o, g_post_mix, g_pre_ffn, w_up, conv_w, conv_b, w_down,
           g_post_ffn, w_ple, g_ple_gate, w_ple_gate, b_ple_gate, g_post_ple):
    b, s, d = x.shape
    row = lambda v: v.reshape(1, -1)
    o_k, o_v = SB_WIDTH, 2 * SB_WIDTH
    o_ql = 3 * SB_WIDTH
    o_kr = o_ql + Q_LORA_RANK + KV_LORA_RANK
    w_kr_pad = jnp.pad(w_in[:, o_kr:], ((0, 0), (QK_NOPE_DIM, MLA_QK_PAD - QK_NOPE_DIM - QK_ROPE_DIM)))
    w_a = jnp.concatenate([w_in[:, o_k:o_v], w_in[:, o_ql:o_kr]], axis=1).astype(BF16)
    w_bt = jnp.concatenate([w_in[:, :o_k], w_in[:, o_v:o_ql], w_kr_pad], axis=1).T.astype(BF16)
    qk = QK_NOPE_DIM + QK_ROPE_DIM
    w_qupt = jnp.pad(w_q_up.reshape(Q_LORA_RANK, MLA_HEADS, qk),
                     ((0, 0), (0, 0), (0, MLA_QK_PAD - qk))).reshape(Q_LORA_RANK, -1).T.astype(BF16)
    w_kv3 = w_kv_up.reshape(KV_LORA_RANK, MLA_HEADS, QK_NOPE_DIM + V_HEAD_DIM)
    w_kn = jnp.pad(w_kv3[:, :, :QK_NOPE_DIM],
                   ((0, 0), (0, 0), (0, MLA_QK_PAD - QK_NOPE_DIM))).reshape(KV_LORA_RANK, -1).astype(BF16)
    w_vt = w_kv3[:, :, QK_NOPE_DIM:].reshape(KV_LORA_RANK, -1).T.astype(BF16)

    sbqt, sbk, sbvt, mqt, mk, mvt = _proj_call(
        x, posr, row(g_pre_mix), w_a, w_bt, row(g_q_lat), w_qupt, row(g_kv_lat), w_kn, w_vt, invf_col)
    n = _attn_call(sbqt, sbk, sbvt, tri, row(g_grp_sb), mqt, mk, mvt, row(g_grp_mla))
    return _ffn_call(x, n, p_i, w_o.astype(BF16), row(g_post_mix), row(g_pre_ffn), w_up.astype(BF16), conv_w, row(conv_b),
                     w_down.astype(BF16), row(g_post_ffn), w_ple.astype(BF16), row(g_ple_gate),
                     w_ple_gate.astype(BF16), row(b_ple_gate), row(g_post_ple))


def kernel(x, p, positions, w_in, g_pre_mix, g_q_lat, w_q_up, g_kv_lat, w_kv_up, g_grp_sb, g_grp_mla, w_o, g_post_mix, g_pre_ffn, w_up, conv_w, conv_b, w_down, g_post_ffn, w_ple, g_ple_gate, w_ple_gate, b_ple_gate, g_post_ple):
    depth = p.shape[0]
    pos_f = positions.astype(F32)
    posr = pos_f[:, None, :]
    inv_freq = 1.0 / (ROPE_THETA ** (jnp.arange(0, QK_ROPE_DIM, 2, dtype=F32) / QK_ROPE_DIM))
    invf_col = inv_freq.reshape(-1, 1)
    kk = jnp.arange(ATT_TK)
    tri = (kk[None, :] >= kk[:, None]).astype(BF16)
    params = (w_in, g_pre_mix, g_q_lat, w_q_up, g_kv_lat, w_kv_up, g_grp_sb, g_grp_mla, w_o, g_post_mix,
              g_pre_ffn, w_up, conv_w, conv_b, w_down, g_post_ffn, w_ple, g_ple_gate, w_ple_gate,
              b_ple_gate, g_post_ple)
    for i in range(depth):
        x = _layer(x, p[i], posr, invf_col, tri, *[w[i] for w in params])
    return x
```

```python
import jax
import jax.numpy as jnp
from jax import lax
from jax.experimental import pallas as pl
from jax.experimental.pallas import tpu as pltpu

F32 = jnp.float32
BF16 = jnp.bfloat16

D_MODEL = 1024
SB_HEADS = 8
SB_HEAD_DIM = 64
MLA_HEADS = 8
QK_NOPE_DIM = 64
QK_ROPE_DIM = 32
V_HEAD_DIM = 64
Q_LORA_RANK = 256
KV_LORA_RANK = 128
SB_WIDTH = SB_HEADS * SB_HEAD_DIM
MLA_WIDTH = MLA_HEADS * V_HEAD_DIM
D_FF = 2816
CONV_WIDTH = 3
PLE_DIM = 256
ROPE_THETA = 10000.0
NORM_EPS = 1e-6

LANES = 128
MLA_QK_PAD = LANES
MLA_QK_WIDTH = MLA_HEADS * MLA_QK_PAD
ROPE_HALF = QK_ROPE_DIM // 2
SB_SCALE = SB_HEAD_DIM ** -0.5
MLA_SCALE_LOG2E = (QK_NOPE_DIM + QK_ROPE_DIM) ** -0.5 * 1.4426950408889634

PROJ_TM = 512
ATT_TQ = 256
ATT_TK = 256
ATT_UNROLL = 4
FFN_TM = 512
FFN_CH = 256
CONV_HALO = 8
FFN_RING = 11
VMEM_LIMIT = 56 * 1024 * 1024


def _rms(x, g):
    ms = jnp.mean(x * x, axis=-1, keepdims=True)
    return x * lax.rsqrt(ms + NORM_EPS) * g


def _dot(a, b):
    return jnp.dot(a, b, preferred_element_type=F32)


def _dot_nt(a, b):
    return lax.dot_general(a, b, (((1,), (1,)), ((), ())), preferred_element_type=F32)


def _proj_kernel(x_ref, posr_ref, g_pre_ref, w_a_ref, w_bt_ref, g_q_ref, w_qupt_ref,
                 g_kv_ref, w_kn_ref, w_vt_ref, invf_col_ref,
                 sbqt_ref, sbk_ref, sbvt_ref, mqt_ref, mk_ref, mvt_ref):
    h = _rms(x_ref[0], g_pre_ref[...]).astype(BF16)
    pa = _dot(h, w_a_ref[...])
    sbk_ref[0] = pa[:, :SB_WIDTH].astype(BF16)
    pbt = _dot_nt(w_bt_ref[...], h)
    sbqt_ref[0] = (pbt[:SB_WIDTH] * SB_SCALE).astype(BF16)
    sbvt_ref[0] = pbt[SB_WIDTH:2 * SB_WIDTH].astype(BF16)

    angt = invf_col_ref[...] * posr_ref[0]
    cost = jnp.cos(angt)
    sint = jnp.sin(angt)

    def rope_rows(t, r1):
        x1 = t[r1:r1 + ROPE_HALF]
        x2 = t[r1 + ROPE_HALF:r1 + QK_ROPE_DIM]
        return x1 * cost - x2 * sint, x2 * cost + x1 * sint

    c0 = SB_WIDTH
    nq = _rms(pa[:, c0:c0 + Q_LORA_RANK], g_q_ref[...]).astype(BF16)
    qt = _dot_nt(w_qupt_ref[...], nq)
    for hd in range(MLA_HEADS):
        r0 = hd * MLA_QK_PAD
        r1 = r0 + QK_NOPE_DIM
        y1, y2 = rope_rows(qt, r1)
        mqt_ref[0, r0:r1, :] = (qt[r0:r1] * MLA_SCALE_LOG2E).astype(BF16)
        mqt_ref[0, r1:r1 + ROPE_HALF, :] = (y1 * MLA_SCALE_LOG2E).astype(BF16)
        mqt_ref[0, r1 + ROPE_HALF:r1 + QK_ROPE_DIM, :] = (y2 * MLA_SCALE_LOG2E).astype(BF16)
        mqt_ref[0, r1 + QK_ROPE_DIM:r0 + MLA_QK_PAD, :] = qt[r1 + QK_ROPE_DIM:r0 + MLA_QK_PAD].astype(BF16)

    c1 = c0 + Q_LORA_RANK
    nkv = _rms(pa[:, c1:c1 + KV_LORA_RANK], g_kv_ref[...]).astype(BF16)
    mvt_ref[0] = _dot_nt(w_vt_ref[...], nkv).astype(BF16)
    kn = _dot(nkv, w_kn_ref[...])
    krt = pbt[2 * SB_WIDTH:]
    y1, y2 = rope_rows(krt, QK_NOPE_DIM)
    krr = jnp.concatenate([krt[:QK_NOPE_DIM], y1, y2, krt[QK_NOPE_DIM + QK_ROPE_DIM:]], axis=0).T
    for hd in range(MLA_HEADS):
        cols = slice(hd * MLA_QK_PAD, (hd + 1) * MLA_QK_PAD)
        mk_ref[0, :, cols] = (kn[:, cols] + krr).astype(BF16)


def _const_spec(shape):
    return pl.BlockSpec(shape, lambda *_: (0,) * len(shape))


def _proj_call(x, posr, g_pre, w_a, w_bt, g_q, w_qupt, g_kv, w_kn, w_vt, invf_col):
    b, s, d = x.shape
    tm = PROJ_TM
    tok = lambda w: pl.BlockSpec((1, tm, w), lambda i, j: (i, j, 0))
    tok_t = lambda r: pl.BlockSpec((1, r, tm), lambda i, j: (i, 0, j))
    return pl.pallas_call(
        _proj_kernel,
        grid=(b, s // tm),
        in_specs=[tok(d), tok_t(1), _const_spec(g_pre.shape), _const_spec(w_a.shape),
                  _const_spec(w_bt.shape), _const_spec(g_q.shape), _const_spec(w_qupt.shape),
                  _const_spec(g_kv.shape), _const_spec(w_kn.shape), _const_spec(w_vt.shape),
                  _const_spec(invf_col.shape)],
        out_specs=[tok_t(SB_WIDTH), tok(SB_WIDTH), tok_t(SB_WIDTH),
                   tok_t(MLA_QK_WIDTH), tok(MLA_QK_WIDTH), tok_t(MLA_WIDTH)],
        out_shape=[jax.ShapeDtypeStruct((b, SB_WIDTH, s), BF16),
                   jax.ShapeDtypeStruct((b, s, SB_WIDTH), BF16),
                   jax.ShapeDtypeStruct((b, SB_WIDTH, s), BF16),
                   jax.ShapeDtypeStruct((b, MLA_QK_WIDTH, s), BF16),
                   jax.ShapeDtypeStruct((b, s, MLA_QK_WIDTH), BF16),
                   jax.ShapeDtypeStruct((b, MLA_WIDTH, s), BF16)],
        compiler_params=pltpu.CompilerParams(dimension_semantics=("arbitrary", "arbitrary"),
                                             vmem_limit_bytes=VMEM_LIMIT),
        name="proj",
    )(x, posr, g_pre, w_a, w_bt, g_q, w_qupt, g_kv, w_kn, w_vt, invf_col)


def _staged(items):
    nsteps = max(i + delays[-1] for i, (_, delays) in enumerate(items)) + 1
    for step in range(nsteps):
        for i, (fns, delays) in enumerate(items):
            for s in reversed(range(len(fns))):
                if step - delays[s] == i:
                    fns[s]()


def _group_norm(acc_ref, g):
    ot = jnp.concatenate([acc_ref[hd] for hd in range(acc_ref.shape[0])], axis=0)
    return _rms(ot.T, g).astype(BF16)


def _softplus(z):
    neg_abs = pltpu.bitcast(pltpu.bitcast(z, jnp.uint32) | jnp.uint32(0x80000000), F32)
    return jnp.maximum(z, 0.0) + jnp.log(1.0 + jnp.exp(neg_abs))


def _attn_kernel(sqt_ref, sk_ref, svt_ref, tri_ref, gs_ref, mqt_ref, mk_ref, mvt_ref, gm_ref, o_ref,
                 qw_ref, sacc_ref, macc_ref, carry_ref, max_ref, sum_ref):
    tq, tk = ATT_TQ, ATT_TK
    qb = pl.program_id(1)
    row = lax.broadcasted_iota(jnp.int32, (LANES, tq), 0)
    for hd in range(SB_HEADS):
        pair = hd // 2
        blk = sqt_ref[0, pair * LANES:(pair + 1) * LANES, :]
        keep = (row < SB_HEAD_DIM) if hd % 2 == 0 else (row >= SB_HEAD_DIM)
        qw_ref[hd] = jnp.where(keep, blk, jnp.zeros_like(blk))
    tri = tri_ref[...]
    key_i = lax.broadcasted_iota(jnp.int32, (tk, tq), 0)
    qry_i = lax.broadcasted_iota(jnp.int32, (tk, tq), 1)
    strict = key_i < qry_i
    causal = key_i <= qry_i

    def tile_items(k0, diag):
        zts, cts, sts = {}, {}, {}

        def sb_item(hd):
            def logits():
                pair = hd // 2
                kp = sk_ref[0, pl.ds(k0, tk), pair * LANES:(pair + 1) * LANES]
                zts[hd] = _dot(kp, qw_ref[hd])

            def cumsum():
                sp = _softplus(zts[hd])
                if diag:
                    sp = jnp.where(strict, sp, 0.0)
                cts[hd] = _dot(tri, sp.astype(BF16))

            def weigh():
                ct = cts.pop(hd)
                wt = jnp.exp(zts.pop(hd) - ct)
                if diag:
                    wt = jnp.where(strict, wt, 0.0)
                contrib = _dot(svt_ref[0, hd * SB_HEAD_DIM:(hd + 1) * SB_HEAD_DIM, pl.ds(k0, tk)],
                               wt.astype(BF16))
                if diag:
                    sacc_ref[hd] = contrib
                    carry_ref[hd] = ct[0:1, :]
                else:
                    carry = carry_ref[hd]
                    sacc_ref[hd] += contrib * jnp.exp(-carry)
                    carry_ref[hd] = ct[0:1, :] + carry

            return (logits, cumsum, weigh), (0, 3, 6)

        def mla_item(hd):
            def scores():
                cols = slice(hd * MLA_QK_PAD, (hd + 1) * MLA_QK_PAD)
                st = _dot(mk_ref[0, pl.ds(k0, tk), cols], mqt_ref[0, cols, :])
                sts[hd] = jnp.where(causal, st, -jnp.inf) if diag else st

            def weigh():
                st = sts.pop(hd)
                mt = jnp.max(st, axis=0, keepdims=True)
                m_old = None if diag else max_ref[hd]
                m_new = mt if diag else jnp.maximum(m_old, mt)
                p = jnp.exp2(st - m_new)
                ps = jnp.sum(p, axis=0, keepdims=True)
                contrib = _dot(mvt_ref[0, hd * V_HEAD_DIM:(hd + 1) * V_HEAD_DIM, pl.ds(k0, tk)], p.astype(BF16))
                if diag:
                    macc_ref[hd] = contrib
                    sum_ref[hd] = ps
                else:
                    alpha = jnp.exp2(m_old - m_new)
                    macc_ref[hd] = alpha * macc_ref[hd] + contrib
                    sum_ref[hd] = alpha * sum_ref[hd] + ps
                max_ref[hd] = m_new

            return (scores, weigh), (0, 3)

        items = []
        for hd in range(SB_HEADS):
            items.append(sb_item(hd))
            items.append(mla_item(hd))
        return items

    _staged(tile_items(pl.multiple_of(qb * tk, tk), True))

    def quad_body(i, _):
        ks = [pl.multiple_of((qb - 1 - ATT_UNROLL * i - t) * tk, tk) for t in range(ATT_UNROLL)]
        _staged([it for k in ks for it in tile_items(k, False)])
        return 0

    lax.fori_loop(0, qb // ATT_UNROLL, quad_body, 0)
    rem = qb % ATT_UNROLL
    for r in range(1, ATT_UNROLL):
        @pl.when(rem == r)
        def _(r=r):
            _staged([it for t in range(r) for it in tile_items((r - 1 - t) * tk, False)])

    for hd in range(MLA_HEADS):
        macc_ref[hd] = macc_ref[hd] / sum_ref[hd]
    o_ref[0, :, :SB_WIDTH] = _group_norm(sacc_ref, gs_ref[...])
    o_ref[0, :, SB_WIDTH:] = _group_norm(macc_ref, gm_ref[...])


def _attn_call(sbqt, sbk, sbvt, tri, g_sb, mqt, mk, mvt, g_mla):
    b, s, w = sbk.shape
    qw = mk.shape[2]
    tq = ATT_TQ
    qcol = lambda r: pl.BlockSpec((1, r, tq), lambda i, j: (i, 0, j))
    seq = lambda r, c: pl.BlockSpec((1, r, c), lambda i, j: (i, 0, 0))
    return pl.pallas_call(
        _attn_kernel,
        grid=(b, s // tq),
        in_specs=[qcol(w), seq(s, w), seq(w, s), _const_spec(tri.shape), _const_spec(g_sb.shape),
                  qcol(qw), seq(s, qw), seq(w, s), _const_spec(g_mla.shape)],
        out_specs=pl.BlockSpec((1, tq, 2 * w), lambda i, j: (i, j, 0)),
        out_shape=jax.ShapeDtypeStruct((b, s, 2 * w), BF16),
        scratch_shapes=[pltpu.VMEM((SB_HEADS, LANES, tq), BF16),
                        pltpu.VMEM((SB_HEADS, SB_HEAD_DIM, tq), F32),
                        pltpu.VMEM((MLA_HEADS, V_HEAD_DIM, tq), F32),
                        pltpu.VMEM((SB_HEADS, 1, tq), F32),
                        pltpu.VMEM((MLA_HEADS, 1, tq), F32),
                        pltpu.VMEM((MLA_HEADS, 1, tq), F32)],
        compiler_params=pltpu.CompilerParams(dimension_semantics=("arbitrary", "arbitrary"),
                                             vmem_limit_bytes=VMEM_LIMIT),
        name="attn",
    )(sbqt, sbk, sbvt, tri, g_sb, mqt, mk, mvt, g_mla)


def _gelu_tanh(x):
    return 0.5 * x * (1.0 + jnp.tanh(0.7978845608028654 * (x + 0.044715 * (x * x * x))))


def _ffn_kernel(x_ref, n_ref, p_ref, w_o_ref, g_mix_ref, g_pre_ref, w_up_ref, cw_ref, cb_ref, w_down_ref,
                g_post_ref, w_ple_ref, g_gate_ref, w_gate_ref, b_gate_ref, g_ple_ref, o_ref,
                halo_ref, facc_ref, *ubuf_refs):
    tm, ch, hl = FFN_TM, FFN_CH, CONV_HALO
    j = pl.program_id(1)

    @pl.when(j == 0)
    def _():
        halo_ref[...] = jnp.zeros_like(halo_ref)

    o_ref[0] = x_ref[0] + _rms(_dot(n_ref[0], w_o_ref[...]), g_mix_ref[...])
    e = _dot(p_ref[0].astype(BF16), w_ple_ref[...])
    h = _rms(o_ref[0], g_pre_ref[...]).astype(BF16)

    def up(c):
        ubuf_ref = ubuf_refs[c % FFN_RING]
        for part in range(2):
            c0 = part * D_FF + c * ch
            pc = slice(part * ch, (part + 1) * ch)
            ubuf_ref[0:hl, pc] = halo_ref[c, :, pc]
            ubuf_ref[hl:hl + tm, pc] = _dot(h, w_up_ref[:, c0:c0 + ch])
        halo_ref[c] = ubuf_ref[tm:tm + hl, :]

    def down(c):
        ubuf_ref = ubuf_refs[c % FFN_RING]
        ys = []
        for part in range(2):
            c0 = part * D_FF + c * ch
            pc = slice(part * ch, (part + 1) * ch)
            y = cb_ref[:, c0:c0 + ch]
            for k in range(CONV_WIDTH):
                sh = hl - (CONV_WIDTH - 1) + k
                y = y + cw_ref[k:k + 1, c0:c0 + ch] * ubuf_ref[sh:sh + tm, pc]
            ys.append(y)
        act = (_gelu_tanh(ys[0]) * ys[1]).astype(BF16)
        contrib = _dot(act, w_down_ref[c * ch:(c + 1) * ch, :])
        if c == 0:
            facc_ref[...] = contrib
        else:
            facc_ref[...] += contrib

    _staged([((lambda c=c: up(c), lambda c=c: down(c)), (0, FFN_RING - 1)) for c in range(D_FF // ch)])

    x2 = o_ref[0] + _rms(facc_ref[...], g_post_ref[...])
    gl = _dot(_rms(x2, g_gate_ref[...]).astype(BF16), w_gate_ref[...]) + b_gate_ref[...]
    o_ref[0] = x2 + _rms(jax.nn.sigmoid(gl) * e, g_ple_ref[...])


def _ffn_call(x, n, p, w_o, g_mix, g_pre, w_up, cw, cb, w_down, g_post, w_ple, g_gate, w_gate, b_gate, g_ple):
    b, s, d = x.shape
    tm = FFN_TM
    tok = lambda w: pl.BlockSpec((1, tm, w), lambda i, j: (i, j, 0))
    consts = [w_o, g_mix, g_pre, w_up, cw, cb, w_down, g_post, w_ple, g_gate, w_gate, b_gate, g_ple]
    return pl.pallas_call(
        _ffn_kernel,
        grid=(b, s // tm),
        in_specs=[tok(d), tok(n.shape[2]), tok(PLE_DIM)] + [
            pl.BlockSpec(a.shape, lambda *_: (0, 0), pipeline_mode=pl.Buffered(1)) for a in consts],
        out_specs=tok(d),
        out_shape=jax.ShapeDtypeStruct((b, s, d), F32),
        scratch_shapes=[pltpu.VMEM((D_FF // FFN_CH, CONV_HALO, 2 * FFN_CH), F32),
                        pltpu.VMEM((tm, d), F32)]
        + [pltpu.VMEM((tm + CONV_HALO, 2 * FFN_CH), F32)] * FFN_RING,
        compiler_params=pltpu.CompilerParams(dimension_semantics=("arbitrary", "arbitrary"),
                                             vmem_limit_bytes=VMEM_LIMIT),
        name="ffn_ple",
    )(x, n, p, *consts)


def _layer(x, p_i, posr, invf_col, tri, w_in, g_pre_mix, g_q_lat, w_q_up, g_kv_lat,
           w_kv_up, g_grp_sb, g_grp_mla, w_o, g_post_mix, g_pre_ffn, w_up, conv_w, conv_b, w_down,
           g_post_ffn, w_ple, g_ple_gate, w_ple_gate, b_ple_gate, g_post_ple):
    b, s, d = x.shape
    row = lambda v: v.reshape(1, -1)
    o_k, o_v = SB_WIDTH, 2 * SB_WIDTH
    o_ql = 3 * SB_WIDTH
    o_kr = o_ql + Q_LORA_RANK + KV_LORA_RANK
    w_kr_pad = jnp.pad(w_in[:, o_kr:], ((0, 0), (QK_NOPE_DIM, MLA_QK_PAD - QK_NOPE_DIM - QK_ROPE_DIM)))
    w_a = jnp.concatenate([w_in[:, o_k:o_v], w_in[:, o_ql:o_kr]], axis=1).astype(BF16)
    w_bt = jnp.concatenate([w_in[:, :o_k], w_in[:, o_v:o_ql], w_kr_pad], axis=1).T.astype(BF16)
    qk = QK_NOPE_DIM + QK_ROPE_DIM
    w_qupt = jnp.pad(w_q_up.reshape(Q_LORA_RANK, MLA_HEADS, qk),
                     ((0, 0), (0, 0), (0, MLA_QK_PAD - qk))).reshape(Q_LORA_RANK, -1).T.astype(BF16)
    w_kv3 = w_kv_up.reshape(KV_LORA_RANK, MLA_HEADS, QK_NOPE_DIM + V_HEAD_DIM)
    w_kn = jnp.pad(w_kv3[:, :, :QK_NOPE_DIM],
                   ((0, 0), (0, 0), (0, MLA_QK_PAD - QK_NOPE_DIM))).reshape(KV_LORA_RANK, -1).astype(BF16)
    w_vt = w_kv3[:, :, QK_NOPE_DIM:].reshape(KV_LORA_RANK, -1).T.astype(BF16)

    sbqt, sbk, sbvt, mqt, mk, mvt = _proj_call(
        x, posr, row(g_pre_mix), w_a, w_bt, row(g_q_lat), w_qupt, row(g_kv_lat), w_kn, w_vt, invf_col)
    n = _attn_call(sbqt, sbk, sbvt, tri, row(g_grp_sb), mqt, mk, mvt, row(g_grp_mla))
    return _ffn_call(x, n, p_i, w_o.astype(BF16), row(g_post_mix), row(g_pre_ffn), w_up.astype(BF16), conv_w, row(conv_b),
                     w_down.astype(BF16), row(g_post_ffn), w_ple.astype(BF16), row(g_ple_gate),
                     w_ple_gate.astype(BF16), row(b_ple_gate), row(g_post_ple))


def kernel(x, p, positions, w_in, g_pre_mix, g_q_lat, w_q_up, g_kv_lat, w_kv_up, g_grp_sb, g_grp_mla, w_o, g_post_mix, g_pre_ffn, w_up, conv_w, conv_b, w_down, g_post_ffn, w_ple, g_ple_gate, w_ple_gate, b_ple_gate, g_post_ple):
    depth = p.shape[0]
    pos_f = positions.astype(F32)
    posr = pos_f[:, None, :]
    inv_freq = 1.0 / (ROPE_THETA ** (jnp.arange(0, QK_ROPE_DIM, 2, dtype=F32) / QK_ROPE_DIM))
    invf_col = inv_freq.reshape(-1, 1)
    kk = jnp.arange(ATT_TK)
    tri = (kk[None, :] >= kk[:, None]).astype(BF16)
    params = (w_in, g_pre_mix, g_q_lat, w_q_up, g_kv_lat, w_kv_up, g_grp_sb, g_grp_mla, w_o, g_post_mix,
              g_pre_ffn, w_up, conv_w, conv_b, w_down, g_post_ffn, w_ple, g_ple_gate, w_ple_gate,
              b_ple_gate, g_post_ple)
    for i in range(depth):
        x = _layer(x, p[i], posr, invf_col, tri, *[w[i] for w in params])
    return x
```
